```python
import math
import jax, jax.numpy as jnp
from jax import lax
import numpy as np

D_MODEL = 2048
BATCH = 1
SEQ = 16384
DEPTH = 2

N_MIXERS = 2
N_LAYERS_A = (DEPTH + 1) // 2
N_LAYERS_B = DEPTH // 2
RMS_EPS = 1e-6
LN_EPS = 1e-5
CHUNK = 128
GMLP_WIDTH = D_MODEL
GMLP_GROUPS = 16
GMLP_GROUP_DIM = GMLP_WIDTH // GMLP_GROUPS
DIFF_HEADS = D_MODEL // 256
DIFF_HEAD_DIM = 128
DIFF_V_DIM = 2 * DIFF_HEAD_DIM
ROPE_THETA = 500000.0
ROPE_DIM = DIFF_HEAD_DIM // 4
Q_BLOCK = 128
N_EXPERTS = 32
TOP_K = 4
D_FF = D_MODEL
SWIGLU_ALPHA = 1.702
SWIGLU_LIMIT = 7.0
MOE_BLOCK = 128
PLE_DIM = 256

kernel_name = "hybrid_gmlp_diffattn_moe_ple"


def rmsnorm(x, g):
    xf = x.astype(jnp.float32)
    var = jnp.mean(xf * xf, axis=-1, keepdims=True)
    return (xf * lax.rsqrt(var + RMS_EPS)).astype(x.dtype) * g


def layernorm(x, g, b):
    xf = x.astype(jnp.float32)
    mu = jnp.mean(xf, axis=-1, keepdims=True)
    var = jnp.mean(jnp.square(xf - mu), axis=-1, keepdims=True)
    return ((xf - mu) * lax.rsqrt(var + LN_EPS)).astype(x.dtype) * g + b


def gmlp_mixer(h, w_in, ln_g, ln_b, w_s, b_s, w_out):
    B, S, _ = h.shape
    z = jax.nn.gelu(h @ w_in, approximate=False)
    u, v = jnp.split(z, 2, axis=-1)
    v = layernorm(v, ln_g, ln_b)
    nc = S // CHUNK
    v = v.reshape(B, nc, CHUNK, GMLP_GROUPS, GMLP_GROUP_DIM)
    causal = jnp.tril(jnp.ones((CHUNK, CHUNK), dtype=bool))
    w_causal = jnp.where(causal[None], w_s, jnp.zeros((), w_s.dtype))
    sv = jnp.einsum('gts,bcsgd->bctgd', w_causal, v) + b_s.T[None, None, :, :, None]
    y = u * sv.reshape(B, S, GMLP_WIDTH)
    return y @ w_out


def partial_rope(x, pos):
    inv_freq = jnp.power(ROPE_THETA, -jnp.arange(0, ROPE_DIM, 2, dtype=jnp.float32) / ROPE_DIM)
    ang = pos.astype(jnp.float32)[:, None] * inv_freq[None, :]
    cos = jnp.cos(ang)[:, None, :]
    sin = jnp.sin(ang)[:, None, :]
    xr = x[..., :ROPE_DIM].astype(jnp.float32)
    x1, x2 = jnp.split(xr, 2, axis=-1)
    rot = jnp.concatenate([x1 * cos - x2 * sin, x2 * cos + x1 * sin], axis=-1)
    return jnp.concatenate([rot.astype(x.dtype), x[..., ROPE_DIM:]], axis=-1)


def diff_attn_mixer(h, layer_idx, w_qkv, lq1, lk1, lq2, lk2, subln_g, w_out):
    B, S, _ = h.shape
    H, Dh, Dv = DIFF_HEADS, DIFF_HEAD_DIM, DIFF_V_DIM
    qkv = h @ w_qkv
    q, k, v = jnp.split(qkv, [2 * H * Dh, 4 * H * Dh], axis=-1)
    pos = jnp.arange(S)
    q = partial_rope(q.reshape(B, S, 2 * H, Dh), pos) * (Dh ** -0.5)
    k = partial_rope(k.reshape(B, S, 2 * H, Dh), pos)
    q = q.reshape(B, S, H, 2, Dh)
    k = k.reshape(B, S, H, 2, Dh)
    v = v.reshape(B, S, H, Dv)
    lambda_init = 0.8 - 0.6 * math.exp(-0.3 * layer_idx)
    f32 = jnp.float32
    lam = (jnp.exp(jnp.sum(lq1.astype(f32) * lk1.astype(f32)))
           - jnp.exp(jnp.sum(lq2.astype(f32) * lk2.astype(f32))) + lambda_init)
    nqb = S // Q_BLOCK
    q_blocks = q.reshape(B, nqb, Q_BLOCK, H, 2, Dh).transpose(1, 0, 2, 3, 4, 5)
    key_pos = jnp.arange(S)

    def one_block(args):
        qb, start = args
        s = jnp.einsum('bqhcd,bkhcd->bhcqk', qb, k).astype(jnp.float32)
        qpos = start + jnp.arange(Q_BLOCK)
        causal = key_pos[None, :] <= qpos[:, None]
        s = jnp.where(causal, s, -jnp.inf)
        pm = jax.nn.softmax(s, axis=-1)
        a = pm[:, :, 0] - lam * pm[:, :, 1]
        return jnp.einsum('bhqk,bkhd->bqhd', a.astype(v.dtype), v)

    o = lax.map(one_block, (q_blocks, jnp.arange(nqb) * Q_BLOCK))
    o = o.transpose(1, 0, 2, 3, 4).reshape(B, S, H, Dv)
    o = rmsnorm(o, subln_g) * (1.0 - lambda_init)
    return o.reshape(B, S, H * Dv) @ w_out


def moe(h, norm_g, router_w, router_b, w1, b1, w2, b2):
    B, S, D = h.shape
    T = B * S
    E = N_EXPERTS
    xs = rmsnorm(h, norm_g).reshape(T, D)
    logits = (xs @ router_w + router_b).astype(jnp.float32)
    top_v, top_i = lax.top_k(logits, TOP_K)
    gate = jax.nn.softmax(top_v, axis=-1)
    flat_e = top_i.reshape(-1)
    flat_tok = jnp.repeat(jnp.arange(T, dtype=jnp.int32), TOP_K)
    flat_g = gate.reshape(-1)
    order = jnp.argsort(flat_e)
    sorted_e = flat_e[order]
    counts = jnp.zeros((E,), jnp.int32).at[flat_e].add(1)
    padded = (counts + MOE_BLOCK - 1) // MOE_BLOCK * MOE_BLOCK
    start = jnp.cumsum(counts) - counts
    pend = jnp.cumsum(padded)
    pstart = pend - padded
    j = jnp.arange(T * TOP_K, dtype=jnp.int32)
    slot = pstart[sorted_e] + (j - start[sorted_e])
    n_blocks = -(-(T * TOP_K) // MOE_BLOCK) + E
    P = n_blocks * MOE_BLOCK
    slot_tok = jnp.full((P,), T, jnp.int32).at[slot].set(flat_tok[order])
    slot_gate = jnp.zeros((P,), jnp.float32).at[slot].set(flat_g[order])
    block_expert = jnp.minimum(
        jnp.searchsorted(pend, jnp.arange(n_blocks, dtype=jnp.int32) * MOE_BLOCK, side='right'), E - 1)
    x_pad = jnp.concatenate([xs, jnp.zeros((1, D), xs.dtype)], axis=0)
    xb = x_pad[slot_tok].reshape(n_blocks, MOE_BLOCK, D)

    def expert_block(args):
        xe, e = args
        hc = xe @ w1[e] + b1[e]
        glu, lin = jnp.split(hc, 2, axis=-1)
        glu = jnp.minimum(glu, SWIGLU_LIMIT)
        lin = jnp.clip(lin, -SWIGLU_LIMIT, SWIGLU_LIMIT)
        act = glu * jax.nn.sigmoid(SWIGLU_ALPHA * glu) * (lin + 1.0)
        return act @ w2[e] + b2[e]

    yb = lax.map(expert_block, (xb, block_expert)).reshape(P, D)
    yb = yb * slot_gate[:, None].astype(yb.dtype)
    y = jax.ops.segment_sum(yb, slot_tok, num_segments=T + 1)[:T]
    return y.reshape(B, S, D)


def per_layer_embed(h, p_i, w_p, p_norm, gate_norm, gate_w, gate_b):
    e = rmsnorm(p_i @ w_p, p_norm)
    g = jax.nn.sigmoid(rmsnorm(h, gate_norm) @ gate_w + gate_b)
    return h + g * e


def setup_inputs(seed: int = 0) -> dict:
    key = jax.random.key(seed)
    ks = jax.random.split(key, 32)
    f32 = jnp.float32
    D, E, F = D_MODEL, N_EXPERTS, D_FF
    A, Bn = N_LAYERS_A, N_LAYERS_B
    nrm = lambda k, shape, scale: scale * jax.random.normal(k, shape, f32)
    gain = lambda k, shape: 1.0 + 0.02 * jax.random.normal(k, shape, f32)
    return {
        "x": jax.random.normal(ks[0], (BATCH, SEQ, D), f32),
        "p": jax.random.normal(ks[1], (DEPTH, BATCH, SEQ, PLE_DIM), f32),
        "a_norm": gain(ks[2], (A, D)),
        "a_w_in": nrm(ks[3], (A, D, 2 * GMLP_WIDTH), D ** -0.5),
        "a_ln_g": gain(ks[4], (A, GMLP_WIDTH)),
        "a_ln_b": nrm(ks[5], (A, GMLP_WIDTH), 0.02),
        "a_w_s": nrm(ks[6], (A, GMLP_GROUPS, CHUNK, CHUNK), CHUNK ** -0.5),
        "a_b_s": gain(ks[7], (A, GMLP_GROUPS, CHUNK)),
        "a_w_out": nrm(ks[8], (A, GMLP_WIDTH, D), GMLP_WIDTH ** -0.5),
        "b_norm": gain(ks[9], (Bn, D)),
        "b_w_qkv": nrm(ks[10], (Bn, D, 3 * D), D ** -0.5),
        "b_lq1": nrm(ks[11], (Bn, DIFF_HEAD_DIM), 0.1),
        "b_lk1": nrm(ks[12], (Bn, DIFF_HEAD_DIM), 0.1),
        "b_lq2": nrm(ks[13], (Bn, DIFF_HEAD_DIM), 0.1),
        "b_lk2": nrm(ks[14], (Bn, DIFF_HEAD_DIM), 0.1),
        "b_subln": gain(ks[15], (Bn, DIFF_V_DIM)),
        "b_w_out": nrm(ks[16], (Bn, DIFF_HEADS * DIFF_V_DIM, D), D ** -0.5),
        "moe_norm": gain(ks[17], (DEPTH, D)),
        "router_w": nrm(ks[18], (DEPTH, D, E), D ** -0.5),
        "router_b": nrm(ks[19], (DEPTH, E), 0.01),
        "moe_w1": nrm(ks[20], (DEPTH, E, D, 2 * F), D ** -0.5),
        "moe_b1": nrm(ks[21], (DEPTH, E, 2 * F), 0.02),
        "moe_w2": nrm(ks[22], (DEPTH, E, F, D), F ** -0.5),
        "moe_b2": nrm(ks[23], (DEPTH, E, D), 0.02),
        "ple_w": nrm(ks[24], (DEPTH, PLE_DIM, D), PLE_DIM ** -0.5),
        "ple_norm": gain(ks[25], (DEPTH, D)),
        "ple_gate_norm": gain(ks[26], (DEPTH, D)),
        "ple_gate_w": nrm(ks[27], (DEPTH, D, D), D ** -0.5),
        "ple_gate_b": nrm(ks[28], (DEPTH, D), 0.02),
        "final_norm": gain(ks[29], (D,)),
    }


def reference(x, p, a_norm, a_w_in, a_ln_g, a_ln_b, a_w_s, a_b_s, a_w_out,
              b_norm, b_w_qkv, b_lq1, b_lk1, b_lq2, b_lk2, b_subln, b_w_out,
              moe_norm, router_w, router_b, moe_w1, moe_b1, moe_w2, moe_b2,
              ple_w, ple_norm, ple_gate_norm, ple_gate_w, ple_gate_b, final_norm):
    h = x
    for i in range(DEPTH):
        j = i // N_MIXERS
        if i % N_MIXERS == 0:
            h = h + gmlp_mixer(rmsnorm(h, a_norm[j]), a_w_in[j], a_ln_g[j], a_ln_b[j],
                               a_w_s[j], a_b_s[j], a_w_out[j])
        else:
            h = h + diff_attn_mixer(rmsnorm(h, b_norm[j]), i, b_w_qkv[j], b_lq1[j], b_lk1[j],
                                    b_lq2[j], b_lk2[j], b_subln[j], b_w_out[j])
        h = h + moe(h, moe_norm[i], router_w[i], router_b[i],
                    moe_w1[i], moe_b1[i], moe_w2[i], moe_b2[i])
        h = per_layer_embed(h, p[i], ple_w[i], ple_norm[i], ple_gate_norm[i],
                            ple_gate_w[i], ple_gate_b[i])
    return rmsnorm(h, final_norm)
```

```python
import functools
import math

import jax
import jax.numpy as jnp
from jax import lax
from jax.experimental import pallas as pl
from jax.experimental.pallas import tpu as pltpu

F32 = jnp.float32
BF16 = jnp.bfloat16

RMS_EPS = 1e-6
LN_EPS = 1e-5
CHUNK = 128
GMLP_GROUPS = 16
DIFF_HEAD_DIM = 128
DIFF_V_DIM = 256
ROPE_THETA = 500000.0
ROPE_DIM = DIFF_HEAD_DIM // 4
ROPE_HALF = ROPE_DIM // 2
N_EXPERTS = 32
TOP_K = 4
SWIGLU_ALPHA = 1.702
SWIGLU_LIMIT = 7.0

LANES = 128
VMEM_LIMIT_BYTES = 48 * 1024 * 1024

EXPERT_TILE_M = 512
EXPERT_TILE_F = 512


def _params(n_axes, vmem=VMEM_LIMIT_BYTES):
    return pltpu.CompilerParams(
        dimension_semantics=("arbitrary",) * n_axes, vmem_limit_bytes=vmem)


def _rms(x, g):
    var = jnp.mean(x * x, axis=-1, keepdims=True)
    return x * lax.rsqrt(var + RMS_EPS) * g


def _sigmoid(x):
    return 1.0 / (1.0 + jnp.exp(-x))


def _norm_matmul_body(*refs, norm, has_bias, act, has_res):
    it = iter(refs)
    x_ref = next(it)
    g_ref = next(it) if norm else None
    w_ref = next(it)
    b_ref = next(it) if has_bias else None
    r_ref = next(it) if has_res else None
    o_ref = next(it)
    xn_ref = next(it)

    @pl.when(pl.program_id(1) == 0)
    def _():
        x = x_ref[...].astype(F32)
        if norm:
            x = _rms(x, g_ref[...])
        xn_ref[...] = x.astype(BF16)

    acc = jnp.dot(xn_ref[...], w_ref[...], preferred_element_type=F32)
    if has_bias:
        acc = acc + b_ref[...]
    if act == "gelu":
        acc = 0.5 * acc * (1.0 + lax.erf(acc * (1.0 / math.sqrt(2.0))))
    if has_res:
        acc = acc + r_ref[...]
    o_ref[...] = acc.astype(o_ref.dtype)


def _norm_matmul(x, g, w, *, bias=None, act=None, residual=None, out_dtype=F32,
                 tm=1024, tn=512, name="norm_matmul"):
    M, K = x.shape
    N = w.shape[1]
    norm = g is not None
    in_specs = [pl.BlockSpec((tm, K), lambda i, j: (i, 0))]
    args = [x]
    if norm:
        in_specs.append(pl.BlockSpec((1, K), lambda i, j: (0, 0)))
        args.append(g.reshape(1, K))
    in_specs.append(pl.BlockSpec((K, tn), lambda i, j: (0, j)))
    args.append(w)
    if bias is not None:
        in_specs.append(pl.BlockSpec((1, tn), lambda i, j: (0, j)))
        args.append(bias.reshape(1, N))
    if residual is not None:
        in_specs.append(pl.BlockSpec((tm, tn), lambda i, j: (i, j)))
        args.append(residual)
    body = functools.partial(_norm_matmul_body, norm=norm, has_bias=bias is not None,
                             act=act, has_res=residual is not None)
    return pl.pallas_call(
        body,
        out_shape=jax.ShapeDtypeStruct((M, N), out_dtype),
        grid=(M // tm, N // tn),
        in_specs=in_specs,
        out_specs=pl.BlockSpec((tm, tn), lambda i, j: (i, j)),
        scratch_shapes=[pltpu.VMEM((tm, K), BF16)],
        compiler_params=_params(2),
        name=name,
    )(*args)


def _sgu_body(u_ref, v_ref, lng_ref, lnb_ref, ws_ref, bias_ref, wo_ref, h_ref, o_ref,
              wt_ref, y_ref, *, tm):
    @pl.when(pl.program_id(0) == 0)
    def _():
        row = lax.broadcasted_iota(jnp.int32, (CHUNK, CHUNK), 0)
        col = lax.broadcasted_iota(jnp.int32, (CHUNK, CHUNK), 1)
        causal = col <= row
        for gi in range(GMLP_GROUPS):
            wt_ref[gi] = jnp.where(causal, ws_ref[gi], 0.0).astype(BF16)

    v = v_ref[...].astype(F32)
    mu = jnp.mean(v, axis=-1, keepdims=True)
    vc = v - mu
    var = jnp.mean(vc * vc, axis=-1, keepdims=True)
    vn = (vc * lax.rsqrt(var + LN_EPS) * lng_ref[...] + lnb_ref[...]).astype(BF16)
    for c in range(tm // CHUNK):
        rows = slice(c * CHUNK, (c + 1) * CHUNK)
        for gi in range(GMLP_GROUPS):
            cols = slice(gi * LANES, (gi + 1) * LANES)
            sv = jnp.dot(wt_ref[gi], vn[rows, cols], preferred_element_type=F32)
            sv = sv + bias_ref[:, cols]
            y_ref[rows, cols] = (u_ref[rows, cols].astype(F32) * sv).astype(BF16)
    o_ref[...] = h_ref[...] + jnp.dot(y_ref[...], wo_ref[...], preferred_element_type=F32)


def _sgu(z, ln_g, ln_b, w_s, b_s, w_out, h, *, tm=256):
    T, D = h.shape
    W = z.shape[1] // 2
    bias_tile = jnp.repeat(b_s.T, W // GMLP_GROUPS, axis=1)
    return pl.pallas_call(
        functools.partial(_sgu_body, tm=tm),
        out_shape=jax.ShapeDtypeStruct((T, D), F32),
        grid=(T // tm,),
        in_specs=[
            pl.BlockSpec((tm, W), lambda i: (i, 0)),
            pl.BlockSpec((tm, W), lambda i: (i, 1)),
            pl.BlockSpec((1, W), lambda i: (0, 0)),
            pl.BlockSpec((1, W), lambda i: (0, 0)),
            pl.BlockSpec((GMLP_GROUPS, CHUNK, CHUNK), lambda i: (0, 0, 0)),
            pl.BlockSpec((CHUNK, W), lambda i: (0, 0)),
            pl.BlockSpec((W, D), lambda i: (0, 0)),
            pl.BlockSpec((tm, D), lambda i: (i, 0)),
        ],
        out_specs=pl.BlockSpec((tm, D), lambda i: (i, 0)),
        scratch_shapes=[pltpu.VMEM((GMLP_GROUPS, CHUNK, CHUNK), BF16),
                        pltpu.VMEM((tm, W), BF16)],
        compiler_params=_params(1),
        name="sgu",
    )(z, z, ln_g.reshape(1, W), ln_b.reshape(1, W), w_s, bias_tile, w_out, h)


def _qkv_body(x_ref, g_ref, w_ref, c_ref, s1_ref, s2_ref, o_ref, xn_ref, *,
              n_q_tiles, n_qk_tiles, heads_per_tile):
    j = pl.program_id(1)

    @pl.when(j == 0)
    def _():
        xn_ref[...] = _rms(x_ref[...], g_ref[...]).astype(BF16)

    acc = jnp.dot(xn_ref[...], w_ref[...], preferred_element_type=F32)

    @pl.when(j < n_qk_tiles)
    def _():
        scale = jnp.where(j < n_q_tiles, DIFF_HEAD_DIM ** -0.5, 1.0).astype(F32)
        cosf = c_ref[...] * scale
        s1 = s1_ref[...] * scale
        s2 = s2_ref[...] * scale
        for hh in range(heads_per_tile):
            cols = slice(hh * LANES, (hh + 1) * LANES)
            seg = acc[:, cols]
            up = pltpu.roll(seg, LANES - ROPE_HALF, 1)
            dn = pltpu.roll(seg, ROPE_HALF, 1)
            o_ref[:, cols] = (seg * cosf + up * s1 + dn * s2).astype(o_ref.dtype)

    @pl.when(j >= n_qk_tiles)
    def _():
        o_ref[...] = acc.astype(o_ref.dtype)


def _rope_tables(T):
    inv_freq = jnp.power(ROPE_THETA, -jnp.arange(0, ROPE_DIM, 2, dtype=F32) / ROPE_DIM)
    ang = jnp.arange(T, dtype=F32)[:, None] * inv_freq[None, :]
    cos, sin = jnp.cos(ang), jnp.sin(ang)
    zeros = jnp.zeros((T, LANES - ROPE_DIM), F32)
    half0 = jnp.zeros((T, ROPE_HALF), F32)
    c_tab = jnp.concatenate([cos, cos, jnp.ones((T, LANES - ROPE_DIM), F32)], axis=1)
    s1_tab = jnp.concatenate([-sin, half0, zeros], axis=1)
    s2_tab = jnp.concatenate([half0, sin, zeros], axis=1)
    return c_tab, s1_tab, s2_tab


def _qkv_rope(h, g, w, *, tm=1024, tn=512):
    T, D = h.shape
    N = w.shape[1]
    c_tab, s1_tab, s2_tab = _rope_tables(T)
    tab_spec = pl.BlockSpec((tm, LANES), lambda i, j: (i, 0))
    body = functools.partial(_qkv_body, n_q_tiles=D // tn, n_qk_tiles=2 * D // tn,
                             heads_per_tile=tn // LANES)
    return pl.pallas_call(
        body,
        out_shape=jax.ShapeDtypeStruct((T, N), BF16),
        grid=(T // tm, N // tn),
        in_specs=[pl.BlockSpec((tm, D), lambda i, j: (i, 0)),
                  pl.BlockSpec((1, D), lambda i, j: (0, 0)),
                  pl.BlockSpec((D, tn), lambda i, j: (0, j)),
                  tab_spec, tab_spec, tab_spec],
        out_specs=pl.BlockSpec((tm, tn), lambda i, j: (i, j)),
        scratch_shapes=[pltpu.VMEM((tm, D), BF16)],
        compiler_params=_params(2),
        name="qkv_rope",
    )(h, g.reshape(1, D), w, c_tab, s1_tab, s2_tab)


def _diff_attn_body(lq1_ref, lk1_ref, lq2_ref, lk2_ref, g_ref, q_ref, k_ref, v_ref, o_ref,
                    m_ref, l_ref, acc_ref, *, tq, lambda_init):
    qi = pl.program_id(1)
    Dh = DIFF_HEAD_DIM
    m_ref[...] = jnp.full(m_ref.shape, -jnp.inf, F32)
    l_ref[...] = jnp.zeros(l_ref.shape, F32)
    acc_ref[...] = jnp.zeros(acc_ref.shape, F32)

    def chunk(j, masked):
        start = pl.multiple_of(j * tq, tq)
        k = k_ref[pl.ds(start, tq), :]
        v = v_ref[pl.ds(start, tq), :]
        for c in range(2):
            s = lax.dot_general(q_ref[:, c * Dh:(c + 1) * Dh], k[:, c * Dh:(c + 1) * Dh],
                                (((1,), (1,)), ((), ())), preferred_element_type=F32)
            if masked:
                row = lax.broadcasted_iota(jnp.int32, (tq, tq), 0)
                col = lax.broadcasted_iota(jnp.int32, (tq, tq), 1)
                s = jnp.where(col <= row, s, -jnp.inf)
            m_prev = m_ref[c]
            m_new = jnp.maximum(m_prev, jnp.max(s, axis=-1, keepdims=True))
            p = jnp.exp(s - m_new)
            alpha = jnp.exp(m_prev - m_new)
            l_ref[c] = alpha * l_ref[c] + jnp.sum(p, axis=-1, keepdims=True)
            acc_ref[c] = alpha * acc_ref[c] + jnp.dot(p.astype(BF16), v,
                                                      preferred_element_type=F32)
            m_ref[c] = m_new

    def full_chunk(j, carry):
        chunk(j, False)
        return carry

    lax.fori_loop(0, qi, full_chunk, 0)
    chunk(qi, True)

    lam = (jnp.exp(jnp.sum(lq1_ref[...] * lk1_ref[...], axis=-1, keepdims=True))
           - jnp.exp(jnp.sum(lq2_ref[...] * lk2_ref[...], axis=-1, keepdims=True))
           + lambda_init)
    o = acc_ref[0] / l_ref[0] - lam * (acc_ref[1] / l_ref[1])
    o_ref[...] = (_rms(o, g_ref[...]) * (1.0 - lambda_init)).astype(o_ref.dtype)


def _diff_attn(qkv, lq1, lk1, lq2, lk2, subln_g, layer_idx, *, tq=512):
    T = qkv.shape[0]
    Dv = DIFF_V_DIM
    D = qkv.shape[1] // 3
    H = D // Dv
    lambda_init = 0.8 - 0.6 * math.exp(-0.3 * layer_idx)
    vec = pl.BlockSpec((1, DIFF_HEAD_DIM), lambda h, i: (0, 0))
    body = functools.partial(_diff_attn_body, tq=tq, lambda_init=lambda_init)
    return pl.pallas_call(
        body,
        out_shape=jax.ShapeDtypeStruct((T, D), BF16),
        grid=(H, T // tq),
        in_specs=[vec, vec, vec, vec,
                  pl.BlockSpec((1, Dv), lambda h, i: (0, 0)),
                  pl.BlockSpec((tq, Dv), lambda h, i: (i, h)),
                  pl.BlockSpec((T, Dv), lambda h, i: (0, H + h)),
                  pl.BlockSpec((T, Dv), lambda h, i: (0, 2 * H + h))],
        out_specs=pl.BlockSpec((tq, Dv), lambda h, i: (i, h)),
        scratch_shapes=[pltpu.VMEM((2, tq, 1), F32), pltpu.VMEM((2, tq, 1), F32),
                        pltpu.VMEM((2, tq, Dv), F32)],
        compiler_params=_params(2),
        name="diff_attn",
    )(lq1.reshape(1, -1), lk1.reshape(1, -1), lq2.reshape(1, -1), lk2.reshape(1, -1),
      subln_g.reshape(1, Dv), qkv, qkv, qkv)


def _router_body(h_ref, g_ref, rw_ref, rb_ref, xs_ref, mi_ref, mf_ref, cnt_ref, run_ref, *, tm):
    @pl.when(pl.program_id(0) == 0)
    def _():
        run_ref[...] = jnp.zeros(run_ref.shape, F32)

    xs = _rms(h_ref[...], g_ref[...])
    xs_ref[...] = xs
    logits = jnp.dot(xs, rw_ref[...], preferred_element_type=F32,
                     precision=lax.Precision.HIGHEST) + rb_ref[...]
    lane = lax.broadcasted_iota(jnp.int32, (tm, LANES), 1)
    work = jnp.where(lane < N_EXPERTS, logits, -jnp.inf)
    vals, idxs, hots = [], [], []
    for _ in range(TOP_K):
        mx = jnp.max(work, axis=-1, keepdims=True)
        idx = jnp.min(jnp.where(work == mx, lane, LANES), axis=-1, keepdims=True)
        hot = lane == idx
        vals.append(mx)
        idxs.append(idx)
        hots.append(hot)
        work = jnp.where(hot, -jnp.inf, work)
    exps = [jnp.exp(v - vals[0]) for v in vals]
    denom = exps[0] + exps[1] + exps[2] + exps[3]

    sel = jnp.zeros((tm, LANES), F32)
    for hot in hots:
        sel = sel + hot.astype(F32)
    row = lax.broadcasted_iota(jnp.int32, (tm, tm), 0)
    col = lax.broadcasted_iota(jnp.int32, (tm, tm), 1)
    before = (col < row).astype(BF16)
    rank_all = jnp.dot(before, sel.astype(BF16), preferred_element_type=F32) + run_ref[...]
    run_ref[...] = run_ref[...] + jnp.sum(sel, axis=0, keepdims=True)
    cnt_ref[...] = run_ref[...].astype(jnp.int32)

    mi = jnp.zeros((tm, LANES), jnp.int32)
    mf = jnp.zeros((tm, LANES), F32)
    for k in range(TOP_K):
        rank_k = jnp.sum(jnp.where(hots[k], rank_all, 0.0), axis=-1, keepdims=True)
        mi = jnp.where(lane == k, idxs[k], mi)
        mi = jnp.where(lane == TOP_K + k, rank_k.astype(jnp.int32), mi)
        mf = jnp.where(lane == k, exps[k] / denom, mf)
    mi_ref[...] = mi
    mf_ref[...] = mf


def _router(h, g, rw, rb, *, tm=512):
    T, D = h.shape
    rw_pad = jnp.zeros((D, LANES), F32).at[:, :N_EXPERTS].set(rw)
    rb_pad = jnp.zeros((1, LANES), F32).at[0, :N_EXPERTS].set(rb)
    return pl.pallas_call(
        functools.partial(_router_body, tm=tm),
        out_shape=(jax.ShapeDtypeStruct((T, D), F32),
                   jax.ShapeDtypeStruct((T, LANES), jnp.int32),
                   jax.ShapeDtypeStruct((T, LANES), F32),
                   jax.ShapeDtypeStruct((1, LANES), jnp.int32)),
        grid=(T // tm,),
        in_specs=[pl.BlockSpec((tm, D), lambda i: (i, 0)),
                  pl.BlockSpec((1, D), lambda i: (0, 0)),
                  pl.BlockSpec((D, LANES), lambda i: (0, 0)),
                  pl.BlockSpec((1, LANES), lambda i: (0, 0))],
        out_specs=(pl.BlockSpec((tm, D), lambda i: (i, 0)),
                   pl.BlockSpec((tm, LANES), lambda i: (i, 0)),
                   pl.BlockSpec((tm, LANES), lambda i: (i, 0)),
                   pl.BlockSpec((1, LANES), lambda i: (0, 0))),
        scratch_shapes=[pltpu.VMEM((1, LANES), F32)],
        compiler_params=_params(1),
        name="router",
    )(h, g.reshape(1, D), rw_pad, rb_pad)


def _dispatch_body(slot_ref, xs_ref, xb_in_ref, xb_ref, sem, *, tm):
    del xb_in_ref
    base = pl.program_id(0) * (tm * TOP_K)

    def row_copy(r, s):
        return pltpu.make_async_copy(xs_ref.at[pl.ds(r, 1), :], xb_ref.at[pl.ds(s, 1), :], sem)

    def issue(r, carry):
        for k in range(TOP_K):
            row_copy(r, slot_ref[base + r * TOP_K + k]).start()
        return carry

    lax.fori_loop(0, tm, issue, 0)
    for _ in range(TOP_K):
        pltpu.make_async_copy(xs_ref, xb_ref.at[pl.ds(0, tm), :], sem).wait()


def _dispatch(slot_flat, xs, n_rows, *, tm=512):
    T, D = xs.shape
    xb_init = jnp.zeros((n_rows, D), F32)
    return pl.pallas_call(
        functools.partial(_dispatch_body, tm=tm),
        out_shape=jax.ShapeDtypeStruct((n_rows, D), F32),
        grid_spec=pltpu.PrefetchScalarGridSpec(
            num_scalar_prefetch=1,
            grid=(T // tm,),
            in_specs=[pl.BlockSpec((tm, D), lambda i, s: (i, 0)),
                      pl.BlockSpec(memory_space=pl.ANY)],
            out_specs=pl.BlockSpec(memory_space=pl.ANY),
            scratch_shapes=[pltpu.SemaphoreType.DMA],
        ),
        input_output_aliases={2: 0},
        compiler_params=_params(1),
        name="moe_dispatch",
    )(slot_flat, xs, xb_init)


def _expert_body(te_ref, nu_ref, x_ref, w1g_ref, w1l_ref, b1g_ref, b1l_ref, w2_ref, b2_ref,
                 o_ref, xbf_ref, *, nf):
    del te_ref
    t = pl.program_id(0)
    f = pl.program_id(1)
    used = t < nu_ref[0]

    @pl.when(used)
    def _():
        @pl.when(f == 0)
        def _():
            xbf_ref[...] = x_ref[...].astype(BF16)

        x = xbf_ref[...]
        glu = jnp.dot(x, w1g_ref[...], preferred_element_type=F32) + b1g_ref[...]
        lin = jnp.dot(x, w1l_ref[...], preferred_element_type=F32) + b1l_ref[...]
        glu = jnp.minimum(glu, SWIGLU_LIMIT)
        lin = jnp.clip(lin, -SWIGLU_LIMIT, SWIGLU_LIMIT)
        act = glu * _sigmoid(SWIGLU_ALPHA * glu) * (lin + 1.0)
        part = jnp.dot(act.astype(BF16), w2_ref[...], preferred_element_type=F32)

        @pl.when(f == 0)
        def _():
            o_ref[...] = part + b2_ref[...]

        @pl.when(f > 0)
        def _():
            o_ref[...] = o_ref[...] + part

    @pl.when(jnp.logical_and(jnp.logical_not(used), f == nf - 1))
    def _():
        o_ref[...] = jnp.zeros(o_ref.shape, F32)


def _expert_ffn(tile_expert, n_used, xb, w1, b1, w2, b2):
    P, D = xb.shape
    E, _, F2 = w1.shape
    F = F2 // 2
    tm, tf = EXPERT_TILE_M, EXPERT_TILE_F
    nf = F // tf
    n_tiles = P // tm

    def fe(t, f, nu):
        return jnp.where(t < nu[0], f, nf - 1)

    def te_(t, te, nu):
        return te[jnp.minimum(t, nu[0] - 1)]

    in_specs = [
        pl.BlockSpec((tm, D), lambda t, f, te, nu: (jnp.minimum(t, nu[0] - 1), 0)),
        pl.BlockSpec((None, D, tf), lambda t, f, te, nu: (te_(t, te, nu), 0, fe(t, f, nu))),
        pl.BlockSpec((None, D, tf), lambda t, f, te, nu: (te_(t, te, nu), 0, nf + fe(t, f, nu))),
        pl.BlockSpec((None, 1, tf), lambda t, f, te, nu: (te_(t, te, nu), 0, fe(t, f, nu))),
        pl.BlockSpec((None, 1, tf), lambda t, f, te, nu: (te_(t, te, nu), 0, nf + fe(t, f, nu))),
        pl.BlockSpec((None, tf, D), lambda t, f, te, nu: (te_(t, te, nu), fe(t, f, nu), 0)),
        pl.BlockSpec((None, 1, D), lambda t, f, te, nu: (te_(t, te, nu), 0, 0)),
    ]
    return pl.pallas_call(
        functools.partial(_expert_body, nf=nf),
        out_shape=jax.ShapeDtypeStruct((P, D), F32),
        grid_spec=pltpu.PrefetchScalarGridSpec(
            num_scalar_prefetch=2,
            grid=(n_tiles, nf),
            in_specs=in_specs,
            out_specs=pl.BlockSpec((tm, D), lambda t, f, te, nu: (t, 0)),
            scratch_shapes=[pltpu.VMEM((tm, D), BF16)],
        ),
        compiler_params=_params(2),
        name="moe_experts",
    )(tile_expert, n_used, xb, w1, w1, b1.reshape(E, 1, F2), b1.reshape(E, 1, F2),
      w2, b2.reshape(E, 1, D))


def _combine_body(slot_ref, yb_ref, gate_ref, h_ref, o_ref, buf_ref, sem, *, tm):
    base = pl.program_id(0) * (tm * TOP_K)

    def issue(r, carry):
        for k in range(TOP_K):
            s = slot_ref[base + r * TOP_K + k]
            pltpu.make_async_copy(yb_ref.at[pl.ds(s, 1), :],
                                  buf_ref.at[k, pl.ds(r, 1), :], sem).start()
        return carry

    lax.fori_loop(0, tm, issue, 0)
    for k in range(TOP_K):
        pltpu.make_async_copy(yb_ref.at[pl.ds(0, tm), :], buf_ref.at[k], sem).wait()
    acc = h_ref[...]
    gates = gate_ref[...]
    for k in range(TOP_K):
        acc = acc + gates[:, k:k + 1] * buf_ref[k]
    o_ref[...] = acc


def _combine(slot_flat, yb, gates, h, *, tm=256):
    T, D = h.shape
    return pl.pallas_call(
        functools.partial(_combine_body, tm=tm),
        out_shape=jax.ShapeDtypeStruct((T, D), F32),
        grid_spec=pltpu.PrefetchScalarGridSpec(
            num_scalar_prefetch=1,
            grid=(T // tm,),
            in_specs=[pl.BlockSpec(memory_space=pl.ANY),
                      pl.BlockSpec((tm, LANES), lambda i, s: (i, 0)),
                      pl.BlockSpec((tm, D), lambda i, s: (i, 0))],
            out_specs=pl.BlockSpec((tm, D), lambda i, s: (i, 0)),
            scratch_shapes=[pltpu.VMEM((TOP_K, tm, D), F32), pltpu.SemaphoreType.DMA],
        ),
        compiler_params=_params(1),
        name="moe_combine",
    )(slot_flat, yb, gates, h)


def _moe(h, norm_g, router_w, router_b, w1, b1, w2, b2):
    T, D = h.shape
    E, tm = N_EXPERTS, EXPERT_TILE_M
    xs, meta_i, meta_f, cnt = _router(h, norm_g, router_w, router_b)
    expert = meta_i[:, :TOP_K]
    rank = meta_i[:, TOP_K:2 * TOP_K]
    counts = cnt[0, :E]
    padded = (counts + tm - 1) // tm * tm
    pend = jnp.cumsum(padded)
    pstart = pend - padded
    slot_flat = (pstart[expert] + rank).reshape(-1).astype(jnp.int32)
    n_tiles = (T * TOP_K) // tm + E
    tile_expert = jnp.minimum(
        jnp.searchsorted(pend, jnp.arange(n_tiles, dtype=jnp.int32) * tm, side="right"),
        E - 1).astype(jnp.int32)
    n_used = (pend[-1:] // tm).astype(jnp.int32)
    xb = _dispatch(slot_flat, xs, n_tiles * tm)
    yb = _expert_ffn(tile_expert, n_used, xb, w1.astype(BF16), b1, w2.astype(BF16), b2)
    return _combine(slot_flat, yb, meta_f, h)


def _ple_body(*refs, final):
    (h_ref, p_ref, wp_ref, pn_ref, gn_ref, gw_ref, gb_ref) = refs[:7]
    fn_ref = refs[7] if final else None
    o_ref = refs[-1]
    h = h_ref[...]
    e = jnp.dot(p_ref[...].astype(BF16), wp_ref[...], preferred_element_type=F32)
    e = _rms(e, pn_ref[...])
    hn = _rms(h, gn_ref[...]).astype(BF16)
    gate = _sigmoid(jnp.dot(hn, gw_ref[...], preferred_element_type=F32) + gb_ref[...])
    out = h + gate * e
    if final:
        out = _rms(out, fn_ref[...])
    o_ref[...] = out


def _ple(h, p_i, w_p, p_norm, gate_norm, gate_w, gate_b, final_norm=None, *, tm=512):
    T, D = h.shape
    Pd = p_i.shape[1]
    final = final_norm is not None
    vec = pl.BlockSpec((1, D), lambda i: (0, 0))
    in_specs = [pl.BlockSpec((tm, D), lambda i: (i, 0)),
                pl.BlockSpec((tm, Pd), lambda i: (i, 0)),
                pl.BlockSpec((Pd, D), lambda i: (0, 0)),
                vec, vec,
                pl.BlockSpec((D, D), lambda i: (0, 0)),
                vec]
    args = [h, p_i, w_p.astype(BF16), p_norm.reshape(1, D), gate_norm.reshape(1, D),
            gate_w.astype(BF16), gate_b.reshape(1, D)]
    if final:
        in_specs.append(vec)
        args.append(final_norm.reshape(1, D))
    return pl.pallas_call(
        functools.partial(_ple_body, final=final),
        out_shape=jax.ShapeDtypeStruct((T, D), F32),
        grid=(T // tm,),
        in_specs=in_specs,
        out_specs=pl.BlockSpec((tm, D), lambda i: (i, 0)),
        compiler_params=_params(1),
        name="ple",
    )(*args)


def kernel(x, p, a_norm, a_w_in, a_ln_g, a_ln_b, a_w_s, a_b_s, a_w_out, b_norm, b_w_qkv, b_lq1, b_lk1, b_lq2, b_lk2, b_subln, b_w_out, moe_norm, router_w, router_b, moe_w1, moe_b1, moe_w2, moe_b2, ple_w, ple_norm, ple_gate_norm, ple_gate_w, ple_gate_b, final_norm):
    B, S, D = x.shape
    assert B == 1, "attention and chunked mixing treat the row axis as one sequence"
    depth = p.shape[0]
    h = x.reshape(B * S, D)
    for i in range(depth):
        j = i // 2
        if i % 2 == 0:
            z = _norm_matmul(h, a_norm[j], a_w_in[j].astype(BF16), act="gelu",
                             out_dtype=BF16, name="gmlp_in")
            h = _sgu(z, a_ln_g[j], a_ln_b[j], a_w_s[j], a_b_s[j], a_w_out[j].astype(BF16), h)
        else:
            qkv = _qkv_rope(h, b_norm[j], b_w_qkv[j].astype(BF16))
            o = _diff_attn(qkv, b_lq1[j], b_lk1[j], b_lq2[j], b_lk2[j], b_subln[j], i)
            h = _norm_matmul(o, None, b_w_out[j].astype(BF16), residual=h, name="attn_out")
        h = _moe(h, moe_norm[i], router_w[i], router_b[i],
                 moe_w1[i], moe_b1[i], moe_w2[i], moe_b2[i])
        h = _ple(h, p[i].reshape(B * S, -1), ple_w[i], ple_norm[i], ple_gate_norm[i],
                 ple_gate_w[i], ple_gate_b[i],
                 final_norm if i == depth - 1 else None)
    return h.reshape(B, S, D)
```

```python
import functools
import math

import jax
import jax.numpy as jnp
from jax import lax
from jax.experimental import pallas as pl
from jax.experimental.pallas import tpu as pltpu

F32 = jnp.float32
BF16 = jnp.bfloat16

RMS_EPS = 1e-6
LN_EPS = 1e-5
CHUNK = 128
GMLP_GROUPS = 16
DIFF_HEAD_DIM = 128
DIFF_V_DIM = 256
ROPE_THETA = 500000.0
ROPE_DIM = DIFF_HEAD_DIM // 4
ROPE_HALF = ROPE_DIM // 2
N_EXPERTS = 32
TOP_K = 4
SWIGLU_ALPHA = 1.702
SWIGLU_LIMIT = 7.0
Q_SCALE = DIFF_HEAD_DIM ** -0.5 * math.log2(math.e)

LANES = 128
VMEM_LIMIT_BYTES = 48 * 1024 * 1024

EXPERT_TILE_M = 1024
EXPERT_SUB_M = 512
EXPERT_TILE_F = 256
EXPERT_VMEM_LIMIT_BYTES = 56 * 1024 * 1024


def _params(n_axes, vmem=VMEM_LIMIT_BYTES):
    return pltpu.CompilerParams(
        dimension_semantics=("arbitrary",) * n_axes, vmem_limit_bytes=vmem)


def _rms(x, g):
    var = jnp.mean(x * x, axis=-1, keepdims=True)
    return x * lax.rsqrt(var + RMS_EPS) * g


def _sigmoid(x):
    return 1.0 / (1.0 + jnp.exp(-x))


def _pack_bf16_pairs(x):
    n = x.shape[1] // 2
    bits = lax.bitcast_convert_type(x.astype(BF16).astype(F32), jnp.uint32)
    return (bits[:, :n] >> 16) | bits[:, n:]


def _unpack_bf16_pairs(packed):
    lo = lax.bitcast_convert_type(packed << 16, F32)
    hi = lax.bitcast_convert_type(packed & jnp.uint32(0xFFFF0000), F32)
    return lo.astype(BF16), hi.astype(BF16)


def _norm_matmul_body(*refs, norm, has_bias, act, has_res):
    it = iter(refs)
    x_ref = next(it)
    g_ref = next(it) if norm else None
    w_ref = next(it)
    b_ref = next(it) if has_bias else None
    r_ref = next(it) if has_res else None
    o_ref = next(it)
    xn_ref = next(it)

    @pl.when(pl.program_id(1) == 0)
    def _():
        x = x_ref[...].astype(F32)
        if norm:
            x = _rms(x, g_ref[...])
        xn_ref[...] = x.astype(BF16)

    acc = jnp.dot(xn_ref[...], w_ref[...], preferred_element_type=F32)
    if has_bias:
        acc = acc + b_ref[...]
    if act == "gelu":
        acc = 0.5 * acc * (1.0 + lax.erf(acc * (1.0 / math.sqrt(2.0))))
    if has_res:
        acc = acc + r_ref[...]
    o_ref[...] = acc.astype(o_ref.dtype)


def _norm_matmul(x, g, w, *, bias=None, act=None, residual=None, out_dtype=F32,
                 tm=1024, tn=512, name="norm_matmul"):
    M, K = x.shape
    N = w.shape[1]
    norm = g is not None
    in_specs = [pl.BlockSpec((tm, K), lambda i, j: (i, 0))]
    args = [x]
    if norm:
        in_specs.append(pl.BlockSpec((1, K), lambda i, j: (0, 0)))
        args.append(g.reshape(1, K))
    in_specs.append(pl.BlockSpec((K, tn), lambda i, j: (0, j)))
    args.append(w)
    if bias is not None:
        in_specs.append(pl.BlockSpec((1, tn), lambda i, j: (0, j)))
        args.append(bias.reshape(1, N))
    if residual is not None:
        in_specs.append(pl.BlockSpec((tm, tn), lambda i, j: (i, j)))
        args.append(residual)
    body = functools.partial(_norm_matmul_body, norm=norm, has_bias=bias is not None,
                             act=act, has_res=residual is not None)
    return pl.pallas_call(
        body,
        out_shape=jax.ShapeDtypeStruct((M, N), out_dtype),
        grid=(M // tm, N // tn),
        in_specs=in_specs,
        out_specs=pl.BlockSpec((tm, tn), lambda i, j: (i, j)),
        scratch_shapes=[pltpu.VMEM((tm, K), BF16)],
        compiler_params=_params(2),
        name=name,
    )(*args)


def _sgu_body(u_ref, v_ref, lng_ref, lnb_ref, ws_ref, bias_ref, wo_ref, h_ref, o_ref,
              wt_ref, y_ref, *, tm):
    @pl.when(pl.program_id(0) == 0)
    def _():
        row = lax.broadcasted_iota(jnp.int32, (CHUNK, CHUNK), 0)
        col = lax.broadcasted_iota(jnp.int32, (CHUNK, CHUNK), 1)
        causal = col <= row
        for gi in range(GMLP_GROUPS):
            wt_ref[gi] = jnp.where(causal, ws_ref[gi], 0.0).astype(BF16)

    v = v_ref[...].astype(F32)
    mu = jnp.mean(v, axis=-1, keepdims=True)
    vc = v - mu
    var = jnp.mean(vc * vc, axis=-1, keepdims=True)
    vn = (vc * lax.rsqrt(var + LN_EPS) * lng_ref[...] + lnb_ref[...]).astype(BF16)
    for c in range(tm // CHUNK):
        rows = slice(c * CHUNK, (c + 1) * CHUNK)
        for gi in range(GMLP_GROUPS):
            cols = slice(gi * LANES, (gi + 1) * LANES)
            sv = jnp.dot(wt_ref[gi], vn[rows, cols], preferred_element_type=F32)
            sv = sv + bias_ref[:, cols]
            y_ref[rows, cols] = (u_ref[rows, cols].astype(F32) * sv).astype(BF16)
    o_ref[...] = h_ref[...] + jnp.dot(y_ref[...], wo_ref[...], preferred_element_type=F32)


def _sgu(z, ln_g, ln_b, w_s, b_s, w_out, h, *, tm=256):
    T, D = h.shape
    W = z.shape[1] // 2
    bias_tile = jnp.repeat(b_s.T, W // GMLP_GROUPS, axis=1)
    return pl.pallas_call(
        functools.partial(_sgu_body, tm=tm),
        out_shape=jax.ShapeDtypeStruct((T, D), F32),
        grid=(T // tm,),
        in_specs=[
            pl.BlockSpec((tm, W), lambda i: (i, 0)),
            pl.BlockSpec((tm, W), lambda i: (i, 1)),
            pl.BlockSpec((1, W), lambda i: (0, 0)),
            pl.BlockSpec((1, W), lambda i: (0, 0)),
            pl.BlockSpec((GMLP_GROUPS, CHUNK, CHUNK), lambda i: (0, 0, 0)),
            pl.BlockSpec((CHUNK, W), lambda i: (0, 0)),
            pl.BlockSpec((W, D), lambda i: (0, 0)),
            pl.BlockSpec((tm, D), lambda i: (i, 0)),
        ],
        out_specs=pl.BlockSpec((tm, D), lambda i: (i, 0)),
        scratch_shapes=[pltpu.VMEM((GMLP_GROUPS, CHUNK, CHUNK), BF16),
                        pltpu.VMEM((tm, W), BF16)],
        compiler_params=_params(1),
        name="sgu",
    )(z, z, ln_g.reshape(1, W), ln_b.reshape(1, W), w_s, bias_tile, w_out, h)


def _qkv_body(x_ref, g_ref, w_ref, c_ref, s1_ref, s2_ref, o_ref, xn_ref, *,
              n_q_tiles, n_qk_tiles, heads_per_tile):
    j = pl.program_id(1)

    @pl.when(j == 0)
    def _():
        xn_ref[...] = _rms(x_ref[...], g_ref[...]).astype(BF16)

    acc = jnp.dot(xn_ref[...], w_ref[...], preferred_element_type=F32)

    @pl.when(j < n_qk_tiles)
    def _():
        scale = jnp.where(j < n_q_tiles, Q_SCALE, 1.0).astype(F32)
        cosf = c_ref[...] * scale
        s1 = s1_ref[...] * scale
        s2 = s2_ref[...] * scale
        for hh in range(heads_per_tile):
            cols = slice(hh * LANES, (hh + 1) * LANES)
            seg = acc[:, cols]
            up = pltpu.roll(seg, LANES - ROPE_HALF, 1)
            dn = pltpu.roll(seg, ROPE_HALF, 1)
            o_ref[:, cols] = (seg * cosf + up * s1 + dn * s2).astype(o_ref.dtype)

    @pl.when(j >= n_qk_tiles)
    def _():
        o_ref[...] = acc.astype(o_ref.dtype)


def _rope_tables(T):
    inv_freq = jnp.power(ROPE_THETA, -jnp.arange(0, ROPE_DIM, 2, dtype=F32) / ROPE_DIM)
    ang = jnp.arange(T, dtype=F32)[:, None] * inv_freq[None, :]
    cos, sin = jnp.cos(ang), jnp.sin(ang)
    zeros = jnp.zeros((T, LANES - ROPE_DIM), F32)
    half0 = jnp.zeros((T, ROPE_HALF), F32)
    c_tab = jnp.concatenate([cos, cos, jnp.ones((T, LANES - ROPE_DIM), F32)], axis=1)
    s1_tab = jnp.concatenate([-sin, half0, zeros], axis=1)
    s2_tab = jnp.concatenate([half0, sin, zeros], axis=1)
    return c_tab, s1_tab, s2_tab


def _qkv_rope(h, g, w, *, tm=1024, tn=512):
    T, D = h.shape
    N = w.shape[1]
    c_tab, s1_tab, s2_tab = _rope_tables(T)
    tab_spec = pl.BlockSpec((tm, LANES), lambda i, j: (i, 0))
    body = functools.partial(_qkv_body, n_q_tiles=D // tn, n_qk_tiles=2 * D // tn,
                             heads_per_tile=tn // LANES)
    return pl.pallas_call(
        body,
        out_shape=jax.ShapeDtypeStruct((T, N), BF16),
        grid=(T // tm, N // tn),
        in_specs=[pl.BlockSpec((tm, D), lambda i, j: (i, 0)),
                  pl.BlockSpec((1, D), lambda i, j: (0, 0)),
                  pl.BlockSpec((D, tn), lambda i, j: (0, j)),
                  tab_spec, tab_spec, tab_spec],
        out_specs=pl.BlockSpec((tm, tn), lambda i, j: (i, j)),
        scratch_shapes=[pltpu.VMEM((tm, D), BF16)],
        compiler_params=_params(2),
        name="qkv_rope",
    )(h, g.reshape(1, D), w, c_tab, s1_tab, s2_tab)


def _diff_attn_body(lq1_ref, lk1_ref, lq2_ref, lk2_ref, g_ref, q_ref, k_ref, v_ref, o_ref,
                    m0_ref, l0_ref, acc0_ref, m1_ref, l1_ref, acc1_ref, *,
                    tq, wide_blocks, lambda_init):
    qi = pl.program_id(1)
    Dh = DIFF_HEAD_DIM
    state = ((m0_ref, l0_ref, acc0_ref), (m1_ref, l1_ref, acc1_ref))
    for m_ref, l_ref, acc_ref in state:
        m_ref[...] = jnp.full(m_ref.shape, -jnp.inf, F32)
        l_ref[...] = jnp.zeros(l_ref.shape, F32)
        acc_ref[...] = jnp.zeros(acc_ref.shape, F32)

    def chunk(start, width, masked):
        n_rep = width // LANES
        k = k_ref[pl.ds(start, width), :]
        v = v_ref[pl.ds(start, width), :]
        scores = []
        for c in range(2):
            s = lax.dot_general(q_ref[:, c * Dh:(c + 1) * Dh], k[:, c * Dh:(c + 1) * Dh],
                                (((1,), (1,)), ((), ())), preferred_element_type=F32)
            if masked:
                row = lax.broadcasted_iota(jnp.int32, (tq, width), 0)
                col = lax.broadcasted_iota(jnp.int32, (tq, width), 1)
                s = jnp.where(col <= row, s, -jnp.inf)
            scores.append(s)
        for s, (m_ref, l_ref, acc_ref) in zip(scores, state):
            m_prev = m_ref[...]
            m_new = jnp.maximum(m_prev, jnp.max(s, axis=-1, keepdims=True))
            p = jnp.exp2(s - jnp.concatenate([m_new] * n_rep, axis=1))
            alpha = jnp.exp2(m_prev - m_new)
            psum = p[:, :LANES]
            for r in range(1, n_rep):
                psum = psum + p[:, r * LANES:(r + 1) * LANES]
            l_ref[...] = alpha * l_ref[...] + psum
            acc_ref[...] = (jnp.concatenate([alpha] * (DIFF_V_DIM // LANES), axis=1) * acc_ref[...]
                            + jnp.dot(p.astype(BF16), v, preferred_element_type=F32))
            m_ref[...] = m_new

    wide = wide_blocks * tq

    def wide_chunk(j, carry):
        chunk(pl.multiple_of(j * wide, wide), wide, False)
        return carry

    n_wide = qi // wide_blocks
    lax.fori_loop(0, n_wide, wide_chunk, 0)
    for r in range(wide_blocks - 1):
        @pl.when(n_wide * wide_blocks + r < qi)
        def _():
            chunk(pl.multiple_of((n_wide * wide_blocks + r) * tq, tq), tq, False)
    chunk(pl.multiple_of(qi * tq, tq), tq, True)

    lam = (jnp.exp(jnp.sum(lq1_ref[...] * lk1_ref[...], axis=-1, keepdims=True))
           - jnp.exp(jnp.sum(lq2_ref[...] * lk2_ref[...], axis=-1, keepdims=True))
           + lambda_init)
    l0 = jnp.sum(l0_ref[...], axis=-1, keepdims=True)
    l1 = jnp.sum(l1_ref[...], axis=-1, keepdims=True)
    o = acc0_ref[...] / l0 - lam * (acc1_ref[...] / l1)
    o_ref[...] = (_rms(o, g_ref[...]) * (1.0 - lambda_init)).astype(o_ref.dtype)


def _diff_attn(qkv, lq1, lk1, lq2, lk2, subln_g, layer_idx, *, tq=512, wide_blocks=2):
    T = qkv.shape[0]
    Dv = DIFF_V_DIM
    D = qkv.shape[1] // 3
    H = D // Dv
    lambda_init = 0.8 - 0.6 * math.exp(-0.3 * layer_idx)
    vec = pl.BlockSpec((1, DIFF_HEAD_DIM), lambda h, i: (0, 0))
    body = functools.partial(_diff_attn_body, tq=tq, wide_blocks=wide_blocks,
                             lambda_init=lambda_init)
    return pl.pallas_call(
        body,
        out_shape=jax.ShapeDtypeStruct((T, D), BF16),
        grid=(H, T // tq),
        in_specs=[vec, vec, vec, vec,
                  pl.BlockSpec((1, Dv), lambda h, i: (0, 0)),
                  pl.BlockSpec((tq, Dv), lambda h, i: (i, h)),
                  pl.BlockSpec((T, Dv), lambda h, i: (0, H + h)),
                  pl.BlockSpec((T, Dv), lambda h, i: (0, 2 * H + h))],
        out_specs=pl.BlockSpec((tq, Dv), lambda h, i: (i, h)),
        scratch_shapes=[pltpu.VMEM((tq, LANES), F32), pltpu.VMEM((tq, LANES), F32),
                        pltpu.VMEM((tq, Dv), F32)] * 2,
        compiler_params=_params(2),
        name="diff_attn",
    )(lq1.reshape(1, -1), lk1.reshape(1, -1), lq2.reshape(1, -1), lk2.reshape(1, -1),
      subln_g.reshape(1, Dv), qkv, qkv, qkv)


def _router_body(h_ref, g_ref, rw_ref, rb_ref, xs_ref, mi_ref, mf_ref, cnt_ref, run_ref, *, tm):
    @pl.when(pl.program_id(0) == 0)
    def _():
        run_ref[...] = jnp.zeros(run_ref.shape, F32)

    xs = _rms(h_ref[...], g_ref[...])
    xs_ref[...] = _pack_bf16_pairs(xs)
    logits = jnp.dot(xs, rw_ref[...], preferred_element_type=F32,
                     precision=lax.Precision.HIGHEST) + rb_ref[...]
    lane = lax.broadcasted_iota(jnp.int32, (tm, LANES), 1)
    work = jnp.where(lane < N_EXPERTS, logits, -jnp.inf)
    vals, idxs, hots = [], [], []
    for _ in range(TOP_K):
        mx = jnp.max(work, axis=-1, keepdims=True)
        idx = jnp.min(jnp.where(work == mx, lane, LANES), axis=-1, keepdims=True)
        hot = lane == idx
        vals.append(mx)
        idxs.append(idx)
        hots.append(hot)
        work = jnp.where(hot, -jnp.inf, work)
    exps = [jnp.exp(v - vals[0]) for v in vals]
    denom = exps[0] + exps[1] + exps[2] + exps[3]

    sel = jnp.zeros((tm, LANES), F32)
    for hot in hots:
        sel = sel + hot.astype(F32)
    row = lax.broadcasted_iota(jnp.int32, (tm, tm), 0)
    col = lax.broadcasted_iota(jnp.int32, (tm, tm), 1)
    before = (col < row).astype(BF16)
    rank_all = jnp.dot(before, sel.astype(BF16), preferred_element_type=F32) + run_ref[...]
    run_ref[...] = run_ref[...] + jnp.sum(sel, axis=0, keepdims=True)
    cnt_ref[...] = run_ref[...].astype(jnp.int32)

    mi = jnp.zeros((tm, LANES), jnp.int32)
    mf = jnp.zeros((tm, LANES), F32)
    for k in range(TOP_K):
        rank_k = jnp.sum(jnp.where(hots[k], rank_all, 0.0), axis=-1, keepdims=True)
        mi = jnp.where(lane == k, idxs[k], mi)
        mi = jnp.where(lane == TOP_K + k, rank_k.astype(jnp.int32), mi)
        mf = jnp.where(lane == k, exps[k] / denom, mf)
    mi_ref[...] = mi
    mf_ref[...] = mf


def _router(h, g, rw, rb, *, tm=512):
    T, D = h.shape
    rw_pad = jnp.zeros((D, LANES), F32).at[:, :N_EXPERTS].set(rw)
    rb_pad = jnp.zeros((1, LANES), F32).at[0, :N_EXPERTS].set(rb)
    return pl.pallas_call(
        functools.partial(_router_body, tm=tm),
        out_shape=(jax.ShapeDtypeStruct((T, D // 2), jnp.uint32),
                   jax.ShapeDtypeStruct((T, LANES), jnp.int32),
                   jax.ShapeDtypeStruct((T, LANES), F32),
                   jax.ShapeDtypeStruct((1, LANES), jnp.int32)),
        grid=(T // tm,),
        in_specs=[pl.BlockSpec((tm, D), lambda i: (i, 0)),
                  pl.BlockSpec((1, D), lambda i: (0, 0)),
                  pl.BlockSpec((D, LANES), lambda i: (0, 0)),
                  pl.BlockSpec((1, LANES), lambda i: (0, 0))],
        out_specs=(pl.BlockSpec((tm, D // 2), lambda i: (i, 0)),
                   pl.BlockSpec((tm, LANES), lambda i: (i, 0)),
                   pl.BlockSpec((tm, LANES), lambda i: (i, 0)),
                   pl.BlockSpec((1, LANES), lambda i: (0, 0))),
        scratch_shapes=[pltpu.VMEM((1, LANES), F32)],
        compiler_params=_params(1),
        name="router",
    )(h, g.reshape(1, D), rw_pad, rb_pad)


def _dispatch_body(slot_ref, xs_ref, xb_in_ref, xb_ref, sem, *, tm):
    del xb_in_ref
    base = pl.program_id(0) * (tm * TOP_K)

    def row_copy(r, s):
        return pltpu.make_async_copy(xs_ref.at[pl.ds(r, 1), :], xb_ref.at[pl.ds(s, 1), :], sem)

    def issue(r, carry):
        for k in range(TOP_K):
            row_copy(r, slot_ref[base + r * TOP_K + k]).start()
        return carry

    lax.fori_loop(0, tm, issue, 0, unroll=4)
    for _ in range(TOP_K):
        pltpu.make_async_copy(xs_ref, xb_ref.at[pl.ds(0, tm), :], sem).wait()


def _dispatch(slot_flat, xs, n_rows, *, tm=512):
    T, D = xs.shape
    xb_init = jnp.zeros((n_rows, D), xs.dtype)
    return pl.pallas_call(
        functools.partial(_dispatch_body, tm=tm),
        out_shape=jax.ShapeDtypeStruct((n_rows, D), xs.dtype),
        grid_spec=pltpu.PrefetchScalarGridSpec(
            num_scalar_prefetch=1,
            grid=(T // tm,),
            in_specs=[pl.BlockSpec((tm, D), lambda i, s: (i, 0)),
                      pl.BlockSpec(memory_space=pl.ANY)],
            out_specs=pl.BlockSpec(memory_space=pl.ANY),
            scratch_shapes=[pltpu.SemaphoreType.DMA],
        ),
        input_output_aliases={2: 0},
        compiler_params=_params(1),
        name="moe_dispatch",
    )(slot_flat, xs, xb_init)


def _expert_body(te_ref, tv_ref, nu_ref, x_ref, w1g_ref, w1l_ref, b1g_ref, b1l_ref, w2_ref,
                 b2_ref, o_ref, xbf_ref, wg_ref, wl_ref, wd_ref, *, nf, sub):
    del te_ref, nu_ref
    t = pl.program_id(0)
    f = pl.program_id(1)
    valid = tv_ref[t]
    tm, half = x_ref.shape

    @pl.when(jnp.logical_and(f == 0, valid > 0))
    def _():
        lo, hi = _unpack_bf16_pairs(x_ref[...])
        xbf_ref[:, :half] = lo
        xbf_ref[:, half:] = hi

    def sub_block(sb):
        rows = pl.ds(sb * sub, sub)
        live = sb * sub < valid

        @pl.when(jnp.logical_and(live, f == 0))
        def _():
            o_ref[rows, :] = jnp.broadcast_to(b2_ref[...], (sub, o_ref.shape[1]))

        @pl.when(live)
        def _():
            if sb == 0:
                w1g = w1g_ref[...].astype(BF16)
                w1l = w1l_ref[...].astype(BF16)
                w2 = w2_ref[...].astype(BF16)
                if tm > sub:
                    wg_ref[...] = w1g
                    wl_ref[...] = w1l
                    wd_ref[...] = w2
            else:
                w1g, w1l, w2 = wg_ref[...], wl_ref[...], wd_ref[...]
            x = xbf_ref[rows, :]
            glu = jnp.dot(x, w1g, preferred_element_type=F32) + b1g_ref[...]
            lin = jnp.dot(x, w1l, preferred_element_type=F32) + b1l_ref[...]
            glu = jnp.minimum(glu, SWIGLU_LIMIT)
            lin = jnp.clip(lin, -SWIGLU_LIMIT, SWIGLU_LIMIT)
            act = glu * _sigmoid(SWIGLU_ALPHA * glu) * (lin + 1.0)
            o_ref[rows, :] += jnp.dot(act.astype(BF16), w2, preferred_element_type=F32)

        @pl.when(jnp.logical_and(jnp.logical_not(live), f == nf - 1))
        def _():
            o_ref[rows, :] = jnp.zeros((sub, o_ref.shape[1]), F32)

    for sb in range(tm // sub):
        sub_block(sb)


def _expert_ffn(tile_expert, tile_valid, n_used, xb, w1, b1, w2, b2, layer):
    P, Dh = xb.shape
    D = 2 * Dh
    _, E, _, F2 = w1.shape
    F = F2 // 2
    tm, tf = EXPERT_TILE_M, EXPERT_TILE_F
    nf = F // tf
    n_tiles = P // tm
    b1r = b1.reshape(-1, E, 1, F2)
    b2r = b2.reshape(-1, E, 1, D)

    def fe(t, f, nu):
        return jnp.where(t < nu[0], f, nf - 1)

    def ex(t, te, nu):
        return te[jnp.minimum(t, nu[0] - 1)]

    in_specs = [
        pl.BlockSpec((tm, Dh), lambda t, f, te, tv, nu: (jnp.minimum(t, nu[0] - 1), 0)),
        pl.BlockSpec((None, None, D, tf),
                     lambda t, f, te, tv, nu: (layer, ex(t, te, nu), 0, fe(t, f, nu))),
        pl.BlockSpec((None, None, D, tf),
                     lambda t, f, te, tv, nu: (layer, ex(t, te, nu), 0, nf + fe(t, f, nu))),
        pl.BlockSpec((None, None, 1, tf),
                     lambda t, f, te, tv, nu: (layer, ex(t, te, nu), 0, fe(t, f, nu))),
        pl.BlockSpec((None, None, 1, tf),
                     lambda t, f, te, tv, nu: (layer, ex(t, te, nu), 0, nf + fe(t, f, nu))),
        pl.BlockSpec((None, None, tf, D),
                     lambda t, f, te, tv, nu: (layer, ex(t, te, nu), fe(t, f, nu), 0)),
        pl.BlockSpec((None, None, 1, D),
                     lambda t, f, te, tv, nu: (layer, ex(t, te, nu), 0, 0)),
    ]
    return pl.pallas_call(
        functools.partial(_expert_body, nf=nf, sub=EXPERT_SUB_M),
        out_shape=jax.ShapeDtypeStruct((P, D), F32),
        grid_spec=pltpu.PrefetchScalarGridSpec(
            num_scalar_prefetch=3,
            grid=(n_tiles, nf),
            in_specs=in_specs,
            out_specs=pl.BlockSpec((tm, D), lambda t, f, te, tv, nu: (t, 0)),
            scratch_shapes=[pltpu.VMEM((tm, D), BF16), pltpu.VMEM((D, tf), BF16),
                            pltpu.VMEM((D, tf), BF16), pltpu.VMEM((tf, D), BF16)],
        ),
        compiler_params=_params(2, EXPERT_VMEM_LIMIT_BYTES),
        name="moe_experts",
    )(tile_expert, tile_valid, n_used, xb, w1, w1, b1r, b1r, w2, b2r)


def _combine_body(slot_ref, yb_ref, gate_ref, h_ref, o_ref, buf_ref, sems, *, tm, n_steps):
    i = pl.program_id(0)

    def start_gather(step, buf):
        base = step * (tm * TOP_K)

        def issue(r, carry):
            for k in range(TOP_K):
                s = slot_ref[base + r * TOP_K + k]
                pltpu.make_async_copy(yb_ref.at[pl.ds(s, 1), :],
                                      buf_ref.at[buf, k, pl.ds(r, 1), :], sems.at[buf]).start()
            return carry

        lax.fori_loop(0, tm, issue, 0, unroll=4)

    @pl.when(i == 0)
    def _():
        start_gather(0, 0)

    @pl.when(i + 1 < n_steps)
    def _():
        start_gather(i + 1, (i + 1) % 2)

    cur = i % 2
    for k in range(TOP_K):
        pltpu.make_async_copy(yb_ref.at[pl.ds(0, tm), :], buf_ref.at[cur, k], sems.at[cur]).wait()
    acc = h_ref[...]
    gates = gate_ref[...]
    for k in range(TOP_K):
        acc = acc + gates[:, k:k + 1] * buf_ref[cur, k]
    o_ref[...] = acc


def _combine(slot_flat, yb, gates, h, *, tm=256):
    T, D = h.shape
    n_steps = T // tm
    return pl.pallas_call(
        functools.partial(_combine_body, tm=tm, n_steps=n_steps),
        out_shape=jax.ShapeDtypeStruct((T, D), F32),
        grid_spec=pltpu.PrefetchScalarGridSpec(
            num_scalar_prefetch=1,
            grid=(n_steps,),
            in_specs=[pl.BlockSpec(memory_space=pl.ANY),
                      pl.BlockSpec((tm, LANES), lambda i, s: (i, 0)),
                      pl.BlockSpec((tm, D), lambda i, s: (i, 0))],
            out_specs=pl.BlockSpec((tm, D), lambda i, s: (i, 0)),
            scratch_shapes=[pltpu.VMEM((2, TOP_K, tm, D), F32),
                            pltpu.SemaphoreType.DMA((2,))],
        ),
        compiler_params=_params(1),
        name="moe_combine",
    )(slot_flat, yb, gates, h)


def _moe(h, norm_g, router_w, router_b, w1, b1, w2, b2, layer):
    T, D = h.shape
    E, tm = N_EXPERTS, EXPERT_TILE_M
    xs, meta_i, meta_f, cnt = _router(h, norm_g, router_w, router_b)
    expert = meta_i[:, :TOP_K]
    rank = meta_i[:, TOP_K:2 * TOP_K]
    counts = cnt[0, :E]
    padded = (counts + tm - 1) // tm * tm
    pend = jnp.cumsum(padded)
    pstart = pend - padded
    slot_flat = (pstart[expert] + rank).reshape(-1).astype(jnp.int32)
    n_tiles = (T * TOP_K) // tm + E
    tile_start = jnp.arange(n_tiles, dtype=jnp.int32) * tm
    tile_expert = jnp.minimum(
        jnp.sum((pend[None, :] <= tile_start[:, None]).astype(jnp.int32), axis=1), E - 1)
    tile_valid = jnp.clip(counts[tile_expert] - (tile_start - pstart[tile_expert]), 0, tm)
    tile_valid = jnp.where(tile_start < pend[-1], tile_valid, 0).astype(jnp.int32)
    n_used = (pend[-1:] // tm).astype(jnp.int32)
    xb = _dispatch(slot_flat, xs, n_tiles * tm)
    yb = _expert_ffn(tile_expert.astype(jnp.int32), tile_valid, n_used, xb, w1, b1, w2, b2, layer)
    return _combine(slot_flat, yb, meta_f, h)


def _ple_body(*refs, final):
    (h_ref, p_ref, wp_ref, pn_ref, gn_ref, gw_ref, gb_ref) = refs[:7]
    fn_ref = refs[7] if final else None
    o_ref = refs[-1]
    h = h_ref[...]
    e = jnp.dot(p_ref[...].astype(BF16), wp_ref[...], preferred_element_type=F32)
    e = _rms(e, pn_ref[...])
    hn = _rms(h, gn_ref[...]).astype(BF16)
    gate = _sigmoid(jnp.dot(hn, gw_ref[...], preferred_element_type=F32) + gb_ref[...])
    out = h + gate * e
    if final:
        out = _rms(out, fn_ref[...])
    o_ref[...] = out


def _ple(h, p_i, w_p, p_norm, gate_norm, gate_w, gate_b, final_norm=None, *, tm=512):
    T, D = h.shape
    Pd = p_i.shape[1]
    final = final_norm is not None
    vec = pl.BlockSpec((1, D), lambda i: (0, 0))
    in_specs = [pl.BlockSpec((tm, D), lambda i: (i, 0)),
                pl.BlockSpec((tm, Pd), lambda i: (i, 0)),
                pl.BlockSpec((Pd, D), lambda i: (0, 0)),
                vec, vec,
                pl.BlockSpec((D, D), lambda i: (0, 0)),
                vec]
    args = [h, p_i, w_p.astype(BF16), p_norm.reshape(1, D), gate_norm.reshape(1, D),
            gate_w.astype(BF16), gate_b.reshape(1, D)]
    if final:
        in_specs.append(vec)
        args.append(final_norm.reshape(1, D))
    return pl.pallas_call(
        functools.partial(_ple_body, final=final),
        out_shape=jax.ShapeDtypeStruct((T, D), F32),
        grid=(T // tm,),
        in_specs=in_specs,
        out_specs=pl.BlockSpec((tm, D), lambda i: (i, 0)),
        compiler_params=_params(1),
        name="ple",
    )(*args)


def kernel(x, p, a_norm, a_w_in, a_ln_g, a_ln_b, a_w_s, a_b_s, a_w_out, b_norm, b_w_qkv, b_lq1, b_lk1, b_lq2, b_lk2, b_subln, b_w_out, moe_norm, router_w, router_b, moe_w1, moe_b1, moe_w2, moe_b2, ple_w, ple_norm, ple_gate_norm, ple_gate_w, ple_gate_b, final_norm):
    B, S, D = x.shape
    assert B == 1, "attention and chunked mixing treat the row axis as one sequence"
    depth = p.shape[0]
    h = x.reshape(B * S, D)
    for i in range(depth):
        j = i // 2
        if i % 2 == 0:
            z = _norm_matmul(h, a_norm[j], a_w_in[j].astype(BF16), act="gelu",
                             out_dtype=BF16, name="gmlp_in")
            h = _sgu(z, a_ln_g[j], a_ln_b[j], a_w_s[j], a_b_s[j], a_w_out[j].astype(BF16), h)
        else:
            qkv = _qkv_rope(h, b_norm[j], b_w_qkv[j].astype(BF16))
            o = _diff_attn(qkv, b_lq1[j], b_lk1[j], b_lq2[j], b_lk2[j], b_subln[j], i)
            h = _norm_matmul(o, None, b_w_out[j].astype(BF16), residual=h, name="attn_out")
        h = _moe(h, moe_norm[i], router_w[i], router_b[i], moe_w1, moe_b1, moe_w2, moe_b2, i)
        h = _ple(h, p[i].reshape(B * S, -1), ple_w[i], ple_norm[i], ple_gate_norm[i],
                 ple_gate_w[i], ple_gate_b[i],
                 final_norm if i == depth - 1 else None)
    return h.reshape(B, S, D)
```

```python
import functools
import math

import jax
import jax.numpy as jnp
from jax import lax
from jax.experimental import pallas as pl
from jax.experimental.pallas import tpu as pltpu

F32 = jnp.float32
BF16 = jnp.bfloat16

RMS_EPS = 1e-6
LN_EPS = 1e-5
CHUNK = 128
GMLP_GROUPS = 16
DIFF_HEAD_DIM = 128
DIFF_V_DIM = 256
ROPE_THETA = 500000.0
ROPE_DIM = DIFF_HEAD_DIM // 4
ROPE_HALF = ROPE_DIM // 2
N_EXPERTS = 32
TOP_K = 4
SWIGLU_ALPHA = 1.702
SWIGLU_LIMIT = 7.0
Q_SCALE = DIFF_HEAD_DIM ** -0.5 * math.log2(math.e)

LANES = 128
VMEM_LIMIT_BYTES = 48 * 1024 * 1024

EXPERT_TILE_M = 512
EXPERT_TILE_COLS = 1024
EXPERT_VMEM_LIMIT_BYTES = 56 * 1024 * 1024


def _params(n_axes, vmem=VMEM_LIMIT_BYTES):
    return pltpu.CompilerParams(
        dimension_semantics=("arbitrary",) * n_axes, vmem_limit_bytes=vmem)


def _rms(x, g):
    var = jnp.mean(x * x, axis=-1, keepdims=True)
    return x * lax.rsqrt(var + RMS_EPS) * g


def _sigmoid(x):
    return 1.0 / (1.0 + jnp.exp(-x))


def _pack_bf16_pairs(x):
    n = x.shape[1] // 2
    bits = lax.bitcast_convert_type(x.astype(BF16).astype(F32), jnp.uint32)
    return (bits[:, :n] >> 16) | bits[:, n:]


def _unpack_bf16_pairs(packed):
    lo = lax.bitcast_convert_type(packed << 16, F32)
    hi = lax.bitcast_convert_type(packed & jnp.uint32(0xFFFF0000), F32)
    return lo.astype(BF16), hi.astype(BF16)


def _norm_matmul_body(*refs, norm, has_bias, act, has_res):
    it = iter(refs)
    x_ref = next(it)
    g_ref = next(it) if norm else None
    w_ref = next(it)
    b_ref = next(it) if has_bias else None
    r_ref = next(it) if has_res else None
    o_ref = next(it)
    xn_ref = next(it)

    @pl.when(pl.program_id(1) == 0)
    def _():
        x = x_ref[...].astype(F32)
        if norm:
            x = _rms(x, g_ref[...])
        xn_ref[...] = x.astype(BF16)

    acc = jnp.dot(xn_ref[...], w_ref[...], preferred_element_type=F32)
    if has_bias:
        acc = acc + b_ref[...]
    if act == "gelu":
        acc = 0.5 * acc * (1.0 + lax.erf(acc * (1.0 / math.sqrt(2.0))))
    if has_res:
        acc = acc + r_ref[...]
    o_ref[...] = acc.astype(o_ref.dtype)


def _norm_matmul(x, g, w, *, bias=None, act=None, residual=None, out_dtype=F32,
                 tm=1024, tn=512, name="norm_matmul"):
    M, K = x.shape
    N = w.shape[1]
    norm = g is not None
    in_specs = [pl.BlockSpec((tm, K), lambda i, j: (i, 0))]
    args = [x]
    if norm:
        in_specs.append(pl.BlockSpec((1, K), lambda i, j: (0, 0)))
        args.append(g.reshape(1, K))
    in_specs.append(pl.BlockSpec((K, tn), lambda i, j: (0, j)))
    args.append(w)
    if bias is not None:
        in_specs.append(pl.BlockSpec((1, tn), lambda i, j: (0, j)))
        args.append(bias.reshape(1, N))
    if residual is not None:
        in_specs.append(pl.BlockSpec((tm, tn), lambda i, j: (i, j)))
        args.append(residual)
    body = functools.partial(_norm_matmul_body, norm=norm, has_bias=bias is not None,
                             act=act, has_res=residual is not None)
    return pl.pallas_call(
        body,
        out_shape=jax.ShapeDtypeStruct((M, N), out_dtype),
        grid=(M // tm, N // tn),
        in_specs=in_specs,
        out_specs=pl.BlockSpec((tm, tn), lambda i, j: (i, j)),
        scratch_shapes=[pltpu.VMEM((tm, K), BF16)],
        compiler_params=_params(2),
        name=name,
    )(*args)


def _sgu_body(u_ref, v_ref, lng_ref, lnb_ref, ws_ref, bias_ref, wo_ref, h_ref, o_ref,
              wt_ref, y_ref, *, tm):
    @pl.when(pl.program_id(0) == 0)
    def _():
        row = lax.broadcasted_iota(jnp.int32, (CHUNK, CHUNK), 0)
        col = lax.broadcasted_iota(jnp.int32, (CHUNK, CHUNK), 1)
        causal = col <= row
        for gi in range(GMLP_GROUPS):
            wt_ref[gi] = jnp.where(causal, ws_ref[gi], 0.0).astype(BF16)

    v = v_ref[...].astype(F32)
    mu = jnp.mean(v, axis=-1, keepdims=True)
    vc = v - mu
    var = jnp.mean(vc * vc, axis=-1, keepdims=True)
    vn = (vc * lax.rsqrt(var + LN_EPS) * lng_ref[...] + lnb_ref[...]).astype(BF16)
    for c in range(tm // CHUNK):
        rows = slice(c * CHUNK, (c + 1) * CHUNK)
        for gi in range(GMLP_GROUPS):
            cols = slice(gi * LANES, (gi + 1) * LANES)
            sv = jnp.dot(wt_ref[gi], vn[rows, cols], preferred_element_type=F32)
            sv = sv + bias_ref[:, cols]
            y_ref[rows, cols] = (u_ref[rows, cols].astype(F32) * sv).astype(BF16)
    o_ref[...] = h_ref[...] + jnp.dot(y_ref[...], wo_ref[...], preferred_element_type=F32)


def _sgu(z, ln_g, ln_b, w_s, b_s, w_out, h, *, tm=256):
    T, D = h.shape
    W = z.shape[1] // 2
    bias_tile = jnp.repeat(b_s.T, W // GMLP_GROUPS, axis=1)
    return pl.pallas_call(
        functools.partial(_sgu_body, tm=tm),
        out_shape=jax.ShapeDtypeStruct((T, D), F32),
        grid=(T // tm,),
        in_specs=[
            pl.BlockSpec((tm, W), lambda i: (i, 0)),
            pl.BlockSpec((tm, W), lambda i: (i, 1)),
            pl.BlockSpec((1, W), lambda i: (0, 0)),
            pl.BlockSpec((1, W), lambda i: (0, 0)),
            pl.BlockSpec((GMLP_GROUPS, CHUNK, CHUNK), lambda i: (0, 0, 0)),
            pl.BlockSpec((CHUNK, W), lambda i: (0, 0)),
            pl.BlockSpec((W, D), lambda i: (0, 0)),
            pl.BlockSpec((tm, D), lambda i: (i, 0)),
        ],
        out_specs=pl.BlockSpec((tm, D), lambda i: (i, 0)),
        scratch_shapes=[pltpu.VMEM((GMLP_GROUPS, CHUNK, CHUNK), BF16),
                        pltpu.VMEM((tm, W), BF16)],
        compiler_params=_params(1),
        name="sgu",
    )(z, z, ln_g.reshape(1, W), ln_b.reshape(1, W), w_s, bias_tile, w_out, h)


def _qkv_body(x_ref, g_ref, w_ref, c_ref, s1_ref, s2_ref, o_ref, xn_ref, *, heads_per_tile):
    @pl.when(pl.program_id(1) == 0)
    def _():
        xn_ref[...] = _rms(x_ref[...], g_ref[...]).astype(BF16)

    acc = jnp.dot(xn_ref[...], w_ref[...], preferred_element_type=F32)
    cosf, s1, s2 = c_ref[...], s1_ref[...], s2_ref[...]
    for hh in range(heads_per_tile):
        cols = slice(hh * LANES, (hh + 1) * LANES)
        seg = acc[:, cols]
        up = pltpu.roll(seg, LANES - ROPE_HALF, 1)
        dn = pltpu.roll(seg, ROPE_HALF, 1)
        o_ref[:, cols] = (seg * cosf + up * s1 + dn * s2).astype(o_ref.dtype)


def _rope_tables(T):
    inv_freq = jnp.power(ROPE_THETA, -jnp.arange(0, ROPE_DIM, 2, dtype=F32) / ROPE_DIM)
    ang = jnp.arange(T, dtype=F32)[:, None] * inv_freq[None, :]
    cos, sin = jnp.cos(ang), jnp.sin(ang)
    zeros = jnp.zeros((T, LANES - ROPE_DIM), F32)
    half0 = jnp.zeros((T, ROPE_HALF), F32)
    c_tab = jnp.concatenate([cos, cos, jnp.ones((T, LANES - ROPE_DIM), F32)], axis=1)
    s1_tab = jnp.concatenate([-sin, half0, zeros], axis=1)
    s2_tab = jnp.concatenate([half0, sin, zeros], axis=1)
    none = jnp.zeros((T, LANES), F32)
    return (jnp.stack([c_tab * Q_SCALE, c_tab, jnp.ones((T, LANES), F32)]),
            jnp.stack([s1_tab * Q_SCALE, s1_tab, none]),
            jnp.stack([s2_tab * Q_SCALE, s2_tab, none]))


def _qkv_rope(h, g, w, *, tm=1024, tn=512):
    T, D = h.shape
    N = w.shape[1]
    c_tab, s1_tab, s2_tab = _rope_tables(T)
    n_q, n_qk = D // tn, 2 * D // tn

    def kind(j):
        return jnp.where(j < n_q, 0, jnp.where(j < n_qk, 1, 2))

    tab_spec = pl.BlockSpec((None, tm, LANES), lambda i, j: (kind(j), i, 0))
    body = functools.partial(_qkv_body, heads_per_tile=tn // LANES)
    return pl.pallas_call(
        body,
        out_shape=jax.ShapeDtypeStruct((T, N), BF16),
        grid=(T // tm, N // tn),
        in_specs=[pl.BlockSpec((tm, D), lambda i, j: (i, 0)),
                  pl.BlockSpec((1, D), lambda i, j: (0, 0)),
                  pl.BlockSpec((D, tn), lambda i, j: (0, j)),
                  tab_spec, tab_spec, tab_spec],
        out_specs=pl.BlockSpec((tm, tn), lambda i, j: (i, j)),
        scratch_shapes=[pltpu.VMEM((tm, D), BF16)],
        compiler_params=_params(2),
        name="qkv_rope",
    )(h, g.reshape(1, D), w, c_tab, s1_tab, s2_tab)


def _diff_attn_body(lq1_ref, lk1_ref, lq2_ref, lk2_ref, g_ref, q_ref, k_ref, v_ref, o_ref,
                    m0_ref, l0_ref, acc0_ref, m1_ref, l1_ref, acc1_ref, *,
                    tq, wide_blocks, lambda_init):
    qi = pl.program_id(1)
    Dh = DIFF_HEAD_DIM
    state = ((m0_ref, l0_ref, acc0_ref), (m1_ref, l1_ref, acc1_ref))
    for m_ref, l_ref, acc_ref in state:
        m_ref[...] = jnp.full(m_ref.shape, -jnp.inf, F32)
        l_ref[...] = jnp.zeros(l_ref.shape, F32)
        acc_ref[...] = jnp.zeros(acc_ref.shape, F32)

    def chunk(start, width, masked):
        n_rep = width // LANES
        k = k_ref[pl.ds(start, width), :]
        v = v_ref[pl.ds(start, width), :]
        scores = []
        for c in range(2):
            s = lax.dot_general(q_ref[:, c * Dh:(c + 1) * Dh], k[:, c * Dh:(c + 1) * Dh],
                                (((1,), (1,)), ((), ())), preferred_element_type=F32)
            if masked:
                row = lax.broadcasted_iota(jnp.int32, (tq, width), 0)
                col = lax.broadcasted_iota(jnp.int32, (tq, width), 1)
                s = jnp.where(col <= row, s, -jnp.inf)
            scores.append(s)
        for s, (m_ref, l_ref, acc_ref) in zip(scores, state):
            m_prev = m_ref[...]
            m_new = jnp.maximum(m_prev, jnp.max(s, axis=-1, keepdims=True))
            p = jnp.exp2(s - jnp.concatenate([m_new] * n_rep, axis=1))
            alpha = jnp.exp2(m_prev - m_new)
            psum = p[:, :LANES]
            for r in range(1, n_rep):
                psum = psum + p[:, r * LANES:(r + 1) * LANES]
            l_ref[...] = alpha * l_ref[...] + psum
            acc_ref[...] = (jnp.concatenate([alpha] * (DIFF_V_DIM // LANES), axis=1) * acc_ref[...]
                            + jnp.dot(p.astype(BF16), v, preferred_element_type=F32))
            m_ref[...] = m_new

    wide = wide_blocks * tq

    def wide_chunk(j, carry):
        chunk(pl.multiple_of(j * wide, wide), wide, False)
        return carry

    n_wide = qi // wide_blocks
    lax.fori_loop(0, n_wide, wide_chunk, 0)
    for r in range(wide_blocks - 1):
        @pl.when(n_wide * wide_blocks + r < qi)
        def _():
            chunk(pl.multiple_of((n_wide * wide_blocks + r) * tq, tq), tq, False)
    chunk(pl.multiple_of(qi * tq, tq), tq, True)

    lam = (jnp.exp(jnp.sum(lq1_ref[...] * lk1_ref[...], axis=-1, keepdims=True))
           - jnp.exp(jnp.sum(lq2_ref[...] * lk2_ref[...], axis=-1, keepdims=True))
           + lambda_init)
    l0 = jnp.sum(l0_ref[...], axis=-1, keepdims=True)
    l1 = jnp.sum(l1_ref[...], axis=-1, keepdims=True)
    o = acc0_ref[...] / l0 - lam * (acc1_ref[...] / l1)
    o_ref[...] = (_rms(o, g_ref[...]) * (1.0 - lambda_init)).astype(o_ref.dtype)


def _diff_attn(qkv, lq1, lk1, lq2, lk2, subln_g, layer_idx, *, tq=1024, wide_blocks=1):
    T = qkv.shape[0]
    Dv = DIFF_V_DIM
    D = qkv.shape[1] // 3
    H = D // Dv
    lambda_init = 0.8 - 0.6 * math.exp(-0.3 * layer_idx)
    vec = pl.BlockSpec((1, DIFF_HEAD_DIM), lambda h, i: (0, 0))
    body = functools.partial(_diff_attn_body, tq=tq, wide_blocks=wide_blocks,
                             lambda_init=lambda_init)
    return pl.pallas_call(
        body,
        out_shape=jax.ShapeDtypeStruct((T, D), BF16),
        grid=(H, T // tq),
        in_specs=[vec, vec, vec, vec,
                  pl.BlockSpec((1, Dv), lambda h, i: (0, 0)),
                  pl.BlockSpec((tq, Dv), lambda h, i: (i, h)),
                  pl.BlockSpec((T, Dv), lambda h, i: (0, H + h),
                               pipeline_mode=pl.Buffered(1)),
                  pl.BlockSpec((T, Dv), lambda h, i: (0, 2 * H + h),
                               pipeline_mode=pl.Buffered(1))],
        out_specs=pl.BlockSpec((tq, Dv), lambda h, i: (i, h)),
        scratch_shapes=[pltpu.VMEM((tq, LANES), F32), pltpu.VMEM((tq, LANES), F32),
                        pltpu.VMEM((tq, Dv), F32)] * 2,
        compiler_params=_params(2),
        name="diff_attn",
    )(lq1.reshape(1, -1), lk1.reshape(1, -1), lq2.reshape(1, -1), lk2.reshape(1, -1),
      subln_g.reshape(1, Dv), qkv, qkv, qkv)


def _router_body(h_ref, g_ref, rw_ref, rb_ref, xs_ref, mi_ref, mf_ref, cnt_ref, run_ref, *, tm):
    @pl.when(pl.program_id(0) == 0)
    def _():
        run_ref[...] = jnp.zeros(run_ref.shape, F32)

    xs = _rms(h_ref[...], g_ref[...])
    xs_ref[...] = _pack_bf16_pairs(xs)
    logits = jnp.dot(xs, rw_ref[...], preferred_element_type=F32,
                     precision=lax.Precision.HIGHEST) + rb_ref[...]
    lane = lax.broadcasted_iota(jnp.int32, (tm, LANES), 1)
    work = jnp.where(lane < N_EXPERTS, logits, -jnp.inf)
    vals, idxs, hots = [], [], []
    for _ in range(TOP_K):
        mx = jnp.max(work, axis=-1, keepdims=True)
        idx = jnp.min(jnp.where(work == mx, lane, LANES), axis=-1, keepdims=True)
        hot = lane == idx
        vals.append(mx)
        idxs.append(idx)
        hots.append(hot)
        work = jnp.where(hot, -jnp.inf, work)
    exps = [jnp.exp(v - vals[0]) for v in vals]
    denom = exps[0] + exps[1] + exps[2] + exps[3]

    sel = jnp.zeros((tm, LANES), F32)
    for hot in hots:
        sel = sel + hot.astype(F32)
    row = lax.broadcasted_iota(jnp.int32, (tm, tm), 0)
    col = lax.broadcasted_iota(jnp.int32, (tm, tm), 1)
    before = (col < row).astype(BF16)
    rank_all = jnp.dot(before, sel.astype(BF16), preferred_element_type=F32) + run_ref[...]
    run_ref[...] = run_ref[...] + jnp.sum(sel, axis=0, keepdims=True)
    cnt_ref[...] = run_ref[...].astype(jnp.int32)

    mi = jnp.zeros((tm, LANES), jnp.int32)
    mf = jnp.zeros((tm, LANES), F32)
    for k in range(TOP_K):
        rank_k = jnp.sum(jnp.where(hots[k], rank_all, 0.0), axis=-1, keepdims=True)
        mi = jnp.where(lane == k, idxs[k], mi)
        mi = jnp.where(lane == TOP_K + k, rank_k.astype(jnp.int32), mi)
        mf = jnp.where(lane == k, exps[k] / denom, mf)
    mi_ref[...] = mi
    mf_ref[...] = mf


def _router(h, g, rw, rb, *, tm=512):
    T, D = h.shape
    rw_pad = jnp.zeros((D, LANES), F32).at[:, :N_EXPERTS].set(rw)
    rb_pad = jnp.zeros((1, LANES), F32).at[0, :N_EXPERTS].set(rb)
    return pl.pallas_call(
        functools.partial(_router_body, tm=tm),
        out_shape=(jax.ShapeDtypeStruct((T, D // 2), jnp.uint32),
                   jax.ShapeDtypeStruct((T, LANES), jnp.int32),
                   jax.ShapeDtypeStruct((T, LANES), F32),
                   jax.ShapeDtypeStruct((1, LANES), jnp.int32)),
        grid=(T // tm,),
        in_specs=[pl.BlockSpec((tm, D), lambda i: (i, 0)),
                  pl.BlockSpec((1, D), lambda i: (0, 0)),
                  pl.BlockSpec((D, LANES), lambda i: (0, 0)),
                  pl.BlockSpec((1, LANES), lambda i: (0, 0))],
        out_specs=(pl.BlockSpec((tm, D // 2), lambda i: (i, 0)),
                   pl.BlockSpec((tm, LANES), lambda i: (i, 0)),
                   pl.BlockSpec((tm, LANES), lambda i: (i, 0)),
                   pl.BlockSpec((1, LANES), lambda i: (0, 0))),
        scratch_shapes=[pltpu.VMEM((1, LANES), F32)],
        compiler_params=_params(1),
        name="router",
    )(h, g.reshape(1, D), rw_pad, rb_pad)


def _dispatch_body(slot_ref, xs_ref, xb_in_ref, xb_ref, sem, *, tm):
    del xb_in_ref
    base = pl.program_id(0) * (tm * TOP_K)

    def row_copy(r, s):
        return pltpu.make_async_copy(xs_ref.at[pl.ds(r, 1), :], xb_ref.at[pl.ds(s, 1), :], sem)

    def issue(r, carry):
        for k in range(TOP_K):
            row_copy(r, slot_ref[base + r * TOP_K + k]).start()
        return carry

    lax.fori_loop(0, tm, issue, 0, unroll=4)
    for _ in range(TOP_K):
        pltpu.make_async_copy(xs_ref, xb_ref.at[pl.ds(0, tm), :], sem).wait()


def _dispatch(slot_flat, xs, n_rows, *, tm=512):
    T, D = xs.shape
    xb_init = jnp.zeros((n_rows, D), xs.dtype)
    return pl.pallas_call(
        functools.partial(_dispatch_body, tm=tm),
        out_shape=jax.ShapeDtypeStruct((n_rows, D), xs.dtype),
        grid_spec=pltpu.PrefetchScalarGridSpec(
            num_scalar_prefetch=1,
            grid=(T // tm,),
            in_specs=[pl.BlockSpec((tm, D), lambda i, s: (i, 0)),
                      pl.BlockSpec(memory_space=pl.ANY)],
            out_specs=pl.BlockSpec(memory_space=pl.ANY),
            scratch_shapes=[pltpu.SemaphoreType.DMA],
        ),
        input_output_aliases={2: 0},
        compiler_params=_params(1),
        name="moe_dispatch",
    )(slot_flat, xs, xb_init)


def _expert_steps(tiles, n_col, max_tiles):
    E = tiles.shape[0]
    tend = jnp.cumsum(tiles)
    tstart = tend - tiles
    n_used_steps = n_col * tend[-1]
    s = jnp.arange(n_col * max_tiles, dtype=jnp.int32)
    s_eff = jnp.minimum(s, jnp.maximum(n_used_steps - 1, 0))
    e_s = jnp.minimum(
        jnp.sum((n_col * tend[None, :] <= s_eff[:, None]).astype(jnp.int32), axis=1), E - 1)
    q = s_eff - n_col * tstart[e_s]
    per = jnp.maximum(tiles[e_s], 1)
    wcol = jnp.clip(q // per, 0, n_col - 1)
    r = q - wcol * per
    used = s < n_used_steps
    first = jnp.logical_and(used, r == 0)
    spare = jnp.maximum(s - n_used_steps, 0)
    col = jnp.where(used, wcol, spare % n_col)
    tile = jnp.where(used, tstart[e_s] + r, tend[-1] + spare // n_col)
    i32 = lambda a: a.astype(jnp.int32)
    return i32(e_s), i32(wcol), i32(col), i32(tile), i32(used), i32(first)


def _expert_up_body(ex_ref, wcol_ref, col_ref, tile_ref, used_ref, first_ref, x_ref, w1g_ref,
                    w1l_ref, b1g_ref, b1l_ref, h_ref, wg_ref, wl_ref):
    del ex_ref, wcol_ref, col_ref, tile_ref
    s = pl.program_id(0)

    @pl.when(used_ref[s] == 0)
    def _():
        h_ref[...] = jnp.zeros(h_ref.shape, h_ref.dtype)

    @pl.when(first_ref[s] == 1)
    def _():
        wg_ref[...] = w1g_ref[...].astype(BF16)
        wl_ref[...] = w1l_ref[...].astype(BF16)

    @pl.when(used_ref[s] == 1)
    def _():
        lo, hi = _unpack_bf16_pairs(x_ref[...])
        x = jnp.concatenate([lo, hi], axis=1)
        glu = jnp.dot(x, wg_ref[...], preferred_element_type=F32) + b1g_ref[...]
        lin = jnp.dot(x, wl_ref[...], preferred_element_type=F32) + b1l_ref[...]
        glu = jnp.minimum(glu, SWIGLU_LIMIT)
        lin = jnp.clip(lin, -SWIGLU_LIMIT, SWIGLU_LIMIT)
        h_ref[...] = (glu * _sigmoid(SWIGLU_ALPHA * glu) * (lin + 1.0)).astype(h_ref.dtype)


def _expert_down_body(ex_ref, wcol_ref, col_ref, tile_ref, used_ref, first_ref, h_ref, w2_ref,
                      b2_ref, o_ref, wd_ref):
    del ex_ref, wcol_ref, col_ref, tile_ref
    s = pl.program_id(0)

    @pl.when(used_ref[s] == 0)
    def _():
        o_ref[...] = jnp.zeros(o_ref.shape, o_ref.dtype)

    @pl.when(first_ref[s] == 1)
    def _():
        wd_ref[...] = w2_ref[...].astype(BF16)

    @pl.when(used_ref[s] == 1)
    def _():
        o_ref[...] = jnp.dot(h_ref[...], wd_ref[...], preferred_element_type=F32) + b2_ref[...]


def _expert_ffn(tiles, xb, w1, b1, w2, b2, layer):
    P, Dh = xb.shape
    D = 2 * Dh
    _, E, _, F2 = w1.shape
    F = F2 // 2
    tm, tc = EXPERT_TILE_M, EXPERT_TILE_COLS
    max_tiles = P // tm
    b1r = b1.reshape(-1, E, 1, F2)
    b2r = b2.reshape(-1, E, 1, D)

    nf = F // tc
    steps = _expert_steps(tiles, nf, max_tiles)
    hidden = pl.pallas_call(
        _expert_up_body,
        out_shape=jax.ShapeDtypeStruct((P, F), BF16),
        grid_spec=pltpu.PrefetchScalarGridSpec(
            num_scalar_prefetch=6,
            grid=(nf * max_tiles,),
            in_specs=[
                pl.BlockSpec((tm, Dh), lambda s, ex, wc, col, tl, us, fr: (tl[s], 0)),
                pl.BlockSpec((None, None, D, tc),
                             lambda s, ex, wc, col, tl, us, fr: (layer, ex[s], 0, wc[s])),
                pl.BlockSpec((None, None, D, tc),
                             lambda s, ex, wc, col, tl, us, fr: (layer, ex[s], 0, nf + wc[s])),
                pl.BlockSpec((None, None, 1, tc),
                             lambda s, ex, wc, col, tl, us, fr: (layer, ex[s], 0, wc[s])),
                pl.BlockSpec((None, None, 1, tc),
                             lambda s, ex, wc, col, tl, us, fr: (layer, ex[s], 0, nf + wc[s])),
            ],
            out_specs=pl.BlockSpec((tm, tc), lambda s, ex, wc, col, tl, us, fr: (tl[s], col[s])),
            scratch_shapes=[pltpu.VMEM((D, tc), BF16), pltpu.VMEM((D, tc), BF16)],
        ),
        compiler_params=_params(1, EXPERT_VMEM_LIMIT_BYTES),
        name="moe_up",
    )(*steps, xb, w1, w1, b1r, b1r)

    nd = D // tc
    steps = _expert_steps(tiles, nd, max_tiles)
    return pl.pallas_call(
        _expert_down_body,
        out_shape=jax.ShapeDtypeStruct((P, D), F32),
        grid_spec=pltpu.PrefetchScalarGridSpec(
            num_scalar_prefetch=6,
            grid=(nd * max_tiles,),
            in_specs=[
                pl.BlockSpec((tm, F), lambda s, ex, wc, col, tl, us, fr: (tl[s], 0)),
                pl.BlockSpec((None, None, F, tc),
                             lambda s, ex, wc, col, tl, us, fr: (layer, ex[s], 0, wc[s])),
                pl.BlockSpec((None, None, 1, tc),
                             lambda s, ex, wc, col, tl, us, fr: (layer, ex[s], 0, wc[s])),
            ],
            out_specs=pl.BlockSpec((tm, tc), lambda s, ex, wc, col, tl, us, fr: (tl[s], col[s])),
            scratch_shapes=[pltpu.VMEM((F, tc), BF16)],
        ),
        compiler_params=_params(1, EXPERT_VMEM_LIMIT_BYTES),
        name="moe_down",
    )(*steps, hidden, w2, b2r)


def _combine_body(slot_ref, yb_ref, gate_ref, h_ref, o_ref, buf_ref, sems, *, tm, n_steps):
    i = pl.program_id(0)

    def start_gather(step, buf):
        base = step * (tm * TOP_K)

        def issue(r, carry):
            for k in range(TOP_K):
                s = slot_ref[base + r * TOP_K + k]
                pltpu.make_async_copy(yb_ref.at[pl.ds(s, 1), :],
                                      buf_ref.at[buf, k, pl.ds(r, 1), :], sems.at[buf]).start()
            return carry

        lax.fori_loop(0, tm, issue, 0, unroll=4)

    @pl.when(i == 0)
    def _():
        start_gather(0, 0)

    @pl.when(i + 1 < n_steps)
    def _():
        start_gather(i + 1, (i + 1) % 2)

    cur = i % 2
    for k in range(TOP_K):
        pltpu.make_async_copy(yb_ref.at[pl.ds(0, tm), :], buf_ref.at[cur, k], sems.at[cur]).wait()
    acc = h_ref[...]
    gates = gate_ref[...]
    for k in range(TOP_K):
        acc = acc + gates[:, k:k + 1] * buf_ref[cur, k]
    o_ref[...] = acc


def _combine(slot_flat, yb, gates, h, *, tm=256):
    T, D = h.shape
    n_steps = T // tm
    return pl.pallas_call(
        functools.partial(_combine_body, tm=tm, n_steps=n_steps),
        out_shape=jax.ShapeDtypeStruct((T, D), F32),
        grid_spec=pltpu.PrefetchScalarGridSpec(
            num_scalar_prefetch=1,
            grid=(n_steps,),
            in_specs=[pl.BlockSpec(memory_space=pl.ANY),
                      pl.BlockSpec((tm, LANES), lambda i, s: (i, 0)),
                      pl.BlockSpec((tm, D), lambda i, s: (i, 0))],
            out_specs=pl.BlockSpec((tm, D), lambda i, s: (i, 0)),
            scratch_shapes=[pltpu.VMEM((2, TOP_K, tm, D), F32),
                            pltpu.SemaphoreType.DMA((2,))],
        ),
        compiler_params=_params(1),
        name="moe_combine",
    )(slot_flat, yb, gates, h)


def _moe(h, norm_g, router_w, router_b, w1, b1, w2, b2, layer):
    T, D = h.shape
    E, tm = N_EXPERTS, EXPERT_TILE_M
    xs, meta_i, meta_f, cnt = _router(h, norm_g, router_w, router_b)
    expert = meta_i[:, :TOP_K]
    rank = meta_i[:, TOP_K:2 * TOP_K]
    counts = cnt[0, :E]
    padded = (counts + tm - 1) // tm * tm
    pend = jnp.cumsum(padded)
    pstart = pend - padded
    slot_flat = (pstart[expert] + rank).reshape(-1).astype(jnp.int32)
    max_tiles = (T * TOP_K) // tm + E
    xb = _dispatch(slot_flat, xs, max_tiles * tm)
    yb = _expert_ffn(padded // tm, xb, w1, b1, w2, b2, layer)
    return _combine(slot_flat, yb, meta_f, h)


def _ple_body(*refs, final):
    (h_ref, p_ref, wp_ref, pn_ref, gn_ref, gw_ref, gb_ref) = refs[:7]
    fn_ref = refs[7] if final else None
    o_ref = refs[-1]
    h = h_ref[...]
    e = jnp.dot(p_ref[...].astype(BF16), wp_ref[...], preferred_element_type=F32)
    e = _rms(e, pn_ref[...])
    hn = _rms(h, gn_ref[...]).astype(BF16)
    gate = _sigmoid(jnp.dot(hn, gw_ref[...], preferred_element_type=F32) + gb_ref[...])
    out = h + gate * e
    if final:
        out = _rms(out, fn_ref[...])
    o_ref[...] = out


def _ple(h, p_i, w_p, p_norm, gate_norm, gate_w, gate_b, final_norm=None, *, tm=512):
    T, D = h.shape
    Pd = p_i.shape[1]
    final = final_norm is not None
    vec = pl.BlockSpec((1, D), lambda i: (0, 0))
    in_specs = [pl.BlockSpec((tm, D), lambda i: (i, 0)),
                pl.BlockSpec((tm, Pd), lambda i: (i, 0)),
                pl.BlockSpec((Pd, D), lambda i: (0, 0)),
                vec, vec,
                pl.BlockSpec((D, D), lambda i: (0, 0)),
                vec]
    args = [h, p_i, w_p.astype(BF16), p_norm.reshape(1, D), gate_norm.reshape(1, D),
            gate_w.astype(BF16), gate_b.reshape(1, D)]
    if final:
        in_specs.append(vec)
        args.append(final_norm.reshape(1, D))
    return pl.pallas_call(
        functools.partial(_ple_body, final=final),
        out_shape=jax.ShapeDtypeStruct((T, D), F32),
        grid=(T // tm,),
        in_specs=in_specs,
        out_specs=pl.BlockSpec((tm, D), lambda i: (i, 0)),
        compiler_params=_params(1),
        name="ple",
    )(*args)


def kernel(x, p, a_norm, a_w_in, a_ln_g, a_ln_b, a_w_s, a_b_s, a_w_out, b_norm, b_w_qkv, b_lq1, b_lk1, b_lq2, b_lk2, b_subln, b_w_out, moe_norm, router_w, router_b, moe_w1, moe_b1, moe_w2, moe_b2, ple_w, ple_norm, ple_gate_norm, ple_gate_w, ple_gate_b, final_norm):
    B, S, D = x.shape
    assert B == 1, "attention and chunked mixing treat the row axis as one sequence"
    depth = p.shape[0]
    h = x.reshape(B * S, D)
    for i in range(depth):
        j = i // 2
        if i % 2 == 0:
            z = _norm_matmul(h, a_norm[j], a_w_in[j].astype(BF16), act="gelu",
                             out_dtype=BF16, name="gmlp_in")
            h = _sgu(z, a_ln_g[j], a_ln_b[j], a_w_s[j], a_b_s[j], a_w_out[j].astype(BF16), h)
        else:
            qkv = _qkv_rope(h, b_norm[j], b_w_qkv[j].astype(BF16))
            o = _diff_attn(qkv, b_lq1[j], b_lk1[j], b_lq2[j], b_lk2[j], b_subln[j], i)
            h = _norm_matmul(o, None, b_w_out[j].astype(BF16), residual=h, name="attn_out")
        h = _moe(h, moe_norm[i], router_w[i], router_b[i], moe_w1, moe_b1, moe_w2, moe_b2, i)
        h = _ple(h, p[i].reshape(B * S, -1), ple_w[i], ple_norm[i], ple_gate_norm[i],
                 ple_gate_w[i], ple_gate_b[i],
                 final_norm if i == depth - 1 else None)
    return h.reshape(B, S, D)
```

```python
import functools
import math

import jax
import jax.numpy as jnp
from jax import lax
from jax.experimental import pallas as pl
from jax.experimental.pallas import tpu as pltpu

F32 = jnp.float32
BF16 = jnp.bfloat16

RMS_EPS = 1e-6
LN_EPS = 1e-5
CHUNK = 128
GMLP_GROUPS = 16
DIFF_HEAD_DIM = 128
DIFF_V_DIM = 256
ROPE_THETA = 500000.0
ROPE_DIM = DIFF_HEAD_DIM // 4
ROPE_HALF = ROPE_DIM // 2
N_EXPERTS = 32
TOP_K = 4
SWIGLU_ALPHA = 1.702
SWIGLU_LIMIT = 7.0
Q_SCALE = DIFF_HEAD_DIM ** -0.5 * math.log2(math.e)

LANES = 128
VMEM_LIMIT_BYTES = 48 * 1024 * 1024

EXPERT_TILE_M = 512
EXPERT_TILE_COLS = 1024
EXPERT_VMEM_LIMIT_BYTES = 56 * 1024 * 1024


def _params(n_axes, vmem=VMEM_LIMIT_BYTES):
    return pltpu.CompilerParams(
        dimension_semantics=("arbitrary",) * n_axes, vmem_limit_bytes=vmem)


def _rms(x, g):
    var = jnp.mean(x * x, axis=-1, keepdims=True)
    return x * lax.rsqrt(var + RMS_EPS) * g


def _sigmoid(x):
    return 1.0 / (1.0 + jnp.exp(-x))


def _pack_bf16_pairs(x):
    n = x.shape[1] // 2
    bits = lax.bitcast_convert_type(x.astype(BF16).astype(F32), jnp.uint32)
    return (bits[:, :n] >> 16) | bits[:, n:]


def _unpack_bf16_pairs(packed, dtype=BF16):
    lo = lax.bitcast_convert_type(packed << 16, F32)
    hi = lax.bitcast_convert_type(packed & jnp.uint32(0xFFFF0000), F32)
    return lo.astype(dtype), hi.astype(dtype)


def _norm_matmul_body(*refs, norm, has_bias, act, has_res):
    it = iter(refs)
    x_ref = next(it)
    g_ref = next(it) if norm else None
    w_ref = next(it)
    b_ref = next(it) if has_bias else None
    r_ref = next(it) if has_res else None
    o_ref = next(it)
    xn_ref = next(it)

    @pl.when(pl.program_id(1) == 0)
    def _():
        x = x_ref[...].astype(F32)
        if norm:
            x = _rms(x, g_ref[...])
        xn_ref[...] = x.astype(BF16)

    acc = jnp.dot(xn_ref[...], w_ref[...], preferred_element_type=F32)
    if has_bias:
        acc = acc + b_ref[...]
    if act == "gelu":
        acc = 0.5 * acc * (1.0 + lax.erf(acc * (1.0 / math.sqrt(2.0))))
    if has_res:
        acc = acc + r_ref[...]
    o_ref[...] = acc.astype(o_ref.dtype)


def _norm_matmul(x, g, w, *, bias=None, act=None, residual=None, out_dtype=F32,
                 tm=1024, tn=512, name="norm_matmul"):
    M, K = x.shape
    N = w.shape[1]
    norm = g is not None
    in_specs = [pl.BlockSpec((tm, K), lambda i, j: (i, 0))]
    args = [x]
    if norm:
        in_specs.append(pl.BlockSpec((1, K), lambda i, j: (0, 0)))
        args.append(g.reshape(1, K))
    in_specs.append(pl.BlockSpec((K, tn), lambda i, j: (0, j)))
    args.append(w)
    if bias is not None:
        in_specs.append(pl.BlockSpec((1, tn), lambda i, j: (0, j)))
        args.append(bias.reshape(1, N))
    if residual is not None:
        in_specs.append(pl.BlockSpec((tm, tn), lambda i, j: (i, j)))
        args.append(residual)
    body = functools.partial(_norm_matmul_body, norm=norm, has_bias=bias is not None,
                             act=act, has_res=residual is not None)
    return pl.pallas_call(
        body,
        out_shape=jax.ShapeDtypeStruct((M, N), out_dtype),
        grid=(M // tm, N // tn),
        in_specs=in_specs,
        out_specs=pl.BlockSpec((tm, tn), lambda i, j: (i, j)),
        scratch_shapes=[pltpu.VMEM((tm, K), BF16)],
        compiler_params=_params(2),
        name=name,
    )(*args)


def _sgu_body(u_ref, v_ref, lng_ref, lnb_ref, ws_ref, bias_ref, wo_ref, h_ref, o_ref,
              wt_ref, y_ref, *, tm):
    @pl.when(pl.program_id(0) == 0)
    def _():
        row = lax.broadcasted_iota(jnp.int32, (CHUNK, CHUNK), 0)
        col = lax.broadcasted_iota(jnp.int32, (CHUNK, CHUNK), 1)
        causal = col <= row
        for gi in range(GMLP_GROUPS):
            wt_ref[gi] = jnp.where(causal, ws_ref[gi], 0.0).astype(BF16)

    v = v_ref[...].astype(F32)
    mu = jnp.mean(v, axis=-1, keepdims=True)
    vc = v - mu
    var = jnp.mean(vc * vc, axis=-1, keepdims=True)
    vn = (vc * lax.rsqrt(var + LN_EPS) * lng_ref[...] + lnb_ref[...]).astype(BF16)
    for c in range(tm // CHUNK):
        rows = slice(c * CHUNK, (c + 1) * CHUNK)
        for gi in range(GMLP_GROUPS):
            cols = slice(gi * LANES, (gi + 1) * LANES)
            sv = jnp.dot(wt_ref[gi], vn[rows, cols], preferred_element_type=F32)
            sv = sv + bias_ref[:, cols]
            y_ref[rows, cols] = (u_ref[rows, cols].astype(F32) * sv).astype(BF16)
    o_ref[...] = h_ref[...] + jnp.dot(y_ref[...], wo_ref[...], preferred_element_type=F32)


def _sgu(z, ln_g, ln_b, w_s, b_s, w_out, h, *, tm=256):
    T, D = h.shape
    W = z.shape[1] // 2
    bias_tile = jnp.repeat(b_s.T, W // GMLP_GROUPS, axis=1)
    return pl.pallas_call(
        functools.partial(_sgu_body, tm=tm),
        out_shape=jax.ShapeDtypeStruct((T, D), F32),
        grid=(T // tm,),
        in_specs=[
            pl.BlockSpec((tm, W), lambda i: (i, 0)),
            pl.BlockSpec((tm, W), lambda i: (i, 1)),
            pl.BlockSpec((1, W), lambda i: (0, 0)),
            pl.BlockSpec((1, W), lambda i: (0, 0)),
            pl.BlockSpec((GMLP_GROUPS, CHUNK, CHUNK), lambda i: (0, 0, 0)),
            pl.BlockSpec((CHUNK, W), lambda i: (0, 0)),
            pl.BlockSpec((W, D), lambda i: (0, 0)),
            pl.BlockSpec((tm, D), lambda i: (i, 0)),
        ],
        out_specs=pl.BlockSpec((tm, D), lambda i: (i, 0)),
        scratch_shapes=[pltpu.VMEM((GMLP_GROUPS, CHUNK, CHUNK), BF16),
                        pltpu.VMEM((tm, W), BF16)],
        compiler_params=_params(1),
        name="sgu",
    )(z, z, ln_g.reshape(1, W), ln_b.reshape(1, W), w_s, bias_tile, w_out, h)


def _qkv_body(x_ref, g_ref, w_ref, c_ref, s1_ref, s2_ref, o_ref, xn_ref, *, heads_per_tile):
    @pl.when(pl.program_id(1) == 0)
    def _():
        xn_ref[...] = _rms(x_ref[...], g_ref[...]).astype(BF16)

    acc = jnp.dot(xn_ref[...], w_ref[...], preferred_element_type=F32)
    cosf, s1, s2 = c_ref[...], s1_ref[...], s2_ref[...]
    for hh in range(heads_per_tile):
        cols = slice(hh * LANES, (hh + 1) * LANES)
        seg = acc[:, cols]
        up = pltpu.roll(seg, LANES - ROPE_HALF, 1)
        dn = pltpu.roll(seg, ROPE_HALF, 1)
        o_ref[:, cols] = (seg * cosf + up * s1 + dn * s2).astype(o_ref.dtype)


def _rope_tables(T):
    inv_freq = jnp.power(ROPE_THETA, -jnp.arange(0, ROPE_DIM, 2, dtype=F32) / ROPE_DIM)
    ang = jnp.arange(T, dtype=F32)[:, None] * inv_freq[None, :]
    cos, sin = jnp.cos(ang), jnp.sin(ang)
    zeros = jnp.zeros((T, LANES - ROPE_DIM), F32)
    half0 = jnp.zeros((T, ROPE_HALF), F32)
    c_tab = jnp.concatenate([cos, cos, jnp.ones((T, LANES - ROPE_DIM), F32)], axis=1)
    s1_tab = jnp.concatenate([-sin, half0, zeros], axis=1)
    s2_tab = jnp.concatenate([half0, sin, zeros], axis=1)
    none = jnp.zeros((T, LANES), F32)
    return (jnp.stack([c_tab * Q_SCALE, c_tab, jnp.ones((T, LANES), F32)]),
            jnp.stack([s1_tab * Q_SCALE, s1_tab, none]),
            jnp.stack([s2_tab * Q_SCALE, s2_tab, none]))


def _qkv_rope(h, g, w, *, tm=1024, tn=512):
    T, D = h.shape
    N = w.shape[1]
    c_tab, s1_tab, s2_tab = _rope_tables(T)
    n_q, n_qk = D // tn, 2 * D // tn

    def kind(j):
        return jnp.where(j < n_q, 0, jnp.where(j < n_qk, 1, 2))

    tab_spec = pl.BlockSpec((None, tm, LANES), lambda i, j: (kind(j), i, 0))
    body = functools.partial(_qkv_body, heads_per_tile=tn // LANES)
    return pl.pallas_call(
        body,
        out_shape=jax.ShapeDtypeStruct((T, N), BF16),
        grid=(T // tm, N // tn),
        in_specs=[pl.BlockSpec((tm, D), lambda i, j: (i, 0)),
                  pl.BlockSpec((1, D), lambda i, j: (0, 0)),
                  pl.BlockSpec((D, tn), lambda i, j: (0, j)),
                  tab_spec, tab_spec, tab_spec],
        out_specs=pl.BlockSpec((tm, tn), lambda i, j: (i, j)),
        scratch_shapes=[pltpu.VMEM((tm, D), BF16)],
        compiler_params=_params(2),
        name="qkv_rope",
    )(h, g.reshape(1, D), w, c_tab, s1_tab, s2_tab)


def _diff_attn_body(lq1_ref, lk1_ref, lq2_ref, lk2_ref, g_ref, q_ref, k_ref, v_ref, o_ref,
                    m0_ref, l0_ref, acc0_ref, m1_ref, l1_ref, acc1_ref, *,
                    tq, wide_blocks, lambda_init):
    qi = pl.program_id(1)
    Dh = DIFF_HEAD_DIM
    state = ((m0_ref, l0_ref, acc0_ref), (m1_ref, l1_ref, acc1_ref))
    for m_ref, l_ref, acc_ref in state:
        m_ref[...] = jnp.full(m_ref.shape, -jnp.inf, F32)
        l_ref[...] = jnp.zeros(l_ref.shape, F32)
        acc_ref[...] = jnp.zeros(acc_ref.shape, F32)

    def chunk(start, width, masked):
        n_rep = width // LANES
        k = k_ref[pl.ds(start, width), :]
        v = v_ref[pl.ds(start, width), :]
        scores = []
        for c in range(2):
            s = lax.dot_general(q_ref[:, c * Dh:(c + 1) * Dh], k[:, c * Dh:(c + 1) * Dh],
                                (((1,), (1,)), ((), ())), preferred_element_type=F32)
            if masked:
                row = lax.broadcasted_iota(jnp.int32, (tq, width), 0)
                col = lax.broadcasted_iota(jnp.int32, (tq, width), 1)
                s = jnp.where(col <= row, s, -jnp.inf)
            scores.append(s)
        for s, (m_ref, l_ref, acc_ref) in zip(scores, state):
            m_prev = m_ref[...]
            m_new = jnp.maximum(m_prev, jnp.max(s, axis=-1, keepdims=True))
            p = jnp.exp2(s - jnp.concatenate([m_new] * n_rep, axis=1))
            alpha = jnp.exp2(m_prev - m_new)
            psum = p[:, :LANES]
            for r in range(1, n_rep):
                psum = psum + p[:, r * LANES:(r + 1) * LANES]
            l_ref[...] = alpha * l_ref[...] + psum
            acc_ref[...] = (jnp.concatenate([alpha] * (DIFF_V_DIM // LANES), axis=1) * acc_ref[...]
                            + jnp.dot(p.astype(BF16), v, preferred_element_type=F32))
            m_ref[...] = m_new

    wide = wide_blocks * tq

    def wide_chunk(j, carry):
        chunk(pl.multiple_of(j * wide, wide), wide, False)
        return carry

    n_wide = qi // wide_blocks
    lax.fori_loop(0, n_wide, wide_chunk, 0)
    for r in range(wide_blocks - 1):
        @pl.when(n_wide * wide_blocks + r < qi)
        def _():
            chunk(pl.multiple_of((n_wide * wide_blocks + r) * tq, tq), tq, False)
    chunk(pl.multiple_of(qi * tq, tq), tq, True)

    lam = (jnp.exp(jnp.sum(lq1_ref[...] * lk1_ref[...], axis=-1, keepdims=True))
           - jnp.exp(jnp.sum(lq2_ref[...] * lk2_ref[...], axis=-1, keepdims=True))
           + lambda_init)
    l0 = jnp.sum(l0_ref[...], axis=-1, keepdims=True)
    l1 = jnp.sum(l1_ref[...], axis=-1, keepdims=True)
    o = acc0_ref[...] / l0 - lam * (acc1_ref[...] / l1)
    o_ref[...] = (_rms(o, g_ref[...]) * (1.0 - lambda_init)).astype(o_ref.dtype)


def _diff_attn(qkv, lq1, lk1, lq2, lk2, subln_g, layer_idx, *, tq=1024, wide_blocks=1):
    T = qkv.shape[0]
    Dv = DIFF_V_DIM
    D = qkv.shape[1] // 3
    H = D // Dv
    lambda_init = 0.8 - 0.6 * math.exp(-0.3 * layer_idx)
    vec = pl.BlockSpec((1, DIFF_HEAD_DIM), lambda h, i: (0, 0))
    body = functools.partial(_diff_attn_body, tq=tq, wide_blocks=wide_blocks,
                             lambda_init=lambda_init)
    return pl.pallas_call(
        body,
        out_shape=jax.ShapeDtypeStruct((T, D), BF16),
        grid=(H, T // tq),
        in_specs=[vec, vec, vec, vec,
                  pl.BlockSpec((1, Dv), lambda h, i: (0, 0)),
                  pl.BlockSpec((tq, Dv), lambda h, i: (i, h)),
                  pl.BlockSpec((T, Dv), lambda h, i: (0, H + h),
                               pipeline_mode=pl.Buffered(1)),
                  pl.BlockSpec((T, Dv), lambda h, i: (0, 2 * H + h),
                               pipeline_mode=pl.Buffered(1))],
        out_specs=pl.BlockSpec((tq, Dv), lambda h, i: (i, h)),
        scratch_shapes=[pltpu.VMEM((tq, LANES), F32), pltpu.VMEM((tq, LANES), F32),
                        pltpu.VMEM((tq, Dv), F32)] * 2,
        compiler_params=_params(2),
        name="diff_attn",
    )(lq1.reshape(1, -1), lk1.reshape(1, -1), lq2.reshape(1, -1), lk2.reshape(1, -1),
      subln_g.reshape(1, Dv), qkv, qkv, qkv)


def _router_body(h_ref, g_ref, rw_ref, rb_ref, xs_ref, mi_ref, mf_ref, cnt_ref, run_ref, *, tm):
    @pl.when(pl.program_id(0) == 0)
    def _():
        run_ref[...] = jnp.zeros(run_ref.shape, F32)

    xs = _rms(h_ref[...], g_ref[...])
    xs_ref[...] = _pack_bf16_pairs(xs)
    logits = jnp.dot(xs, rw_ref[...], preferred_element_type=F32,
                     precision=lax.Precision.HIGHEST) + rb_ref[...]
    lane = lax.broadcasted_iota(jnp.int32, (tm, LANES), 1)
    work = jnp.where(lane < N_EXPERTS, logits, -jnp.inf)
    vals, idxs, hots = [], [], []
    for _ in range(TOP_K):
        mx = jnp.max(work, axis=-1, keepdims=True)
        idx = jnp.min(jnp.where(work == mx, lane, LANES), axis=-1, keepdims=True)
        hot = lane == idx
        vals.append(mx)
        idxs.append(idx)
        hots.append(hot)
        work = jnp.where(hot, -jnp.inf, work)
    exps = [jnp.exp(v - vals[0]) for v in vals]
    denom = exps[0] + exps[1] + exps[2] + exps[3]

    sel = jnp.zeros((tm, LANES), F32)
    for hot in hots:
        sel = sel + hot.astype(F32)
    row = lax.broadcasted_iota(jnp.int32, (tm, tm), 0)
    col = lax.broadcasted_iota(jnp.int32, (tm, tm), 1)
    before = (col < row).astype(BF16)
    rank_all = jnp.dot(before, sel.astype(BF16), preferred_element_type=F32) + run_ref[...]
    run_ref[...] = run_ref[...] + jnp.sum(sel, axis=0, keepdims=True)
    cnt_ref[...] = run_ref[...].astype(jnp.int32)

    mi = jnp.zeros((tm, LANES), jnp.int32)
    mf = jnp.zeros((tm, LANES), F32)
    for k in range(TOP_K):
        rank_k = jnp.sum(jnp.where(hots[k], rank_all, 0.0), axis=-1, keepdims=True)
        mi = jnp.where(lane == k, idxs[k], mi)
        mi = jnp.where(lane == TOP_K + k, rank_k.astype(jnp.int32), mi)
        mf = jnp.where(lane == k, exps[k] / denom, mf)
    mi_ref[...] = mi
    mf_ref[...] = mf


def _router(h, g, rw, rb, *, tm=512):
    T, D = h.shape
    rw_pad = jnp.zeros((D, LANES), F32).at[:, :N_EXPERTS].set(rw)
    rb_pad = jnp.zeros((1, LANES), F32).at[0, :N_EXPERTS].set(rb)
    return pl.pallas_call(
        functools.partial(_router_body, tm=tm),
        out_shape=(jax.ShapeDtypeStruct((T, D // 2), jnp.uint32),
                   jax.ShapeDtypeStruct((T, LANES), jnp.int32),
                   jax.ShapeDtypeStruct((T, LANES), F32),
                   jax.ShapeDtypeStruct((1, LANES), jnp.int32)),
        grid=(T // tm,),
        in_specs=[pl.BlockSpec((tm, D), lambda i: (i, 0)),
                  pl.BlockSpec((1, D), lambda i: (0, 0)),
                  pl.BlockSpec((D, LANES), lambda i: (0, 0)),
                  pl.BlockSpec((1, LANES), lambda i: (0, 0))],
        out_specs=(pl.BlockSpec((tm, D // 2), lambda i: (i, 0)),
                   pl.BlockSpec((tm, LANES), lambda i: (i, 0)),
                   pl.BlockSpec((tm, LANES), lambda i: (i, 0)),
                   pl.BlockSpec((1, LANES), lambda i: (0, 0))),
        scratch_shapes=[pltpu.VMEM((1, LANES), F32)],
        compiler_params=_params(1),
        name="router",
    )(h, g.reshape(1, D), rw_pad, rb_pad)


def _dispatch_body(slot_ref, xs_ref, xb_in_ref, xb_ref, sem, *, tm):
    del xb_in_ref
    base = pl.program_id(0) * (tm * TOP_K)

    def row_copy(r, s):
        return pltpu.make_async_copy(xs_ref.at[pl.ds(r, 1), :], xb_ref.at[pl.ds(s, 1), :], sem)

    def issue(r, carry):
        for k in range(TOP_K):
            row_copy(r, slot_ref[base + r * TOP_K + k]).start()
        return carry

    lax.fori_loop(0, tm, issue, 0, unroll=4)
    for _ in range(TOP_K):
        pltpu.make_async_copy(xs_ref, xb_ref.at[pl.ds(0, tm), :], sem).wait()


def _dispatch(slot_flat, xs, n_rows, *, tm=512):
    T, D = xs.shape
    xb_init = jnp.zeros((n_rows, D), xs.dtype)
    return pl.pallas_call(
        functools.partial(_dispatch_body, tm=tm),
        out_shape=jax.ShapeDtypeStruct((n_rows, D), xs.dtype),
        grid_spec=pltpu.PrefetchScalarGridSpec(
            num_scalar_prefetch=1,
            grid=(T // tm,),
            in_specs=[pl.BlockSpec((tm, D), lambda i, s: (i, 0)),
                      pl.BlockSpec(memory_space=pl.ANY)],
            out_specs=pl.BlockSpec(memory_space=pl.ANY),
            scratch_shapes=[pltpu.SemaphoreType.DMA],
        ),
        input_output_aliases={2: 0},
        compiler_params=_params(1),
        name="moe_dispatch",
    )(slot_flat, xs, xb_init)


def _expert_steps(tiles, n_col, max_tiles):
    E = tiles.shape[0]
    tend = jnp.cumsum(tiles)
    tstart = tend - tiles
    n_used_steps = n_col * tend[-1]
    s = jnp.arange(n_col * max_tiles, dtype=jnp.int32)
    s_eff = jnp.minimum(s, jnp.maximum(n_used_steps - 1, 0))
    e_s = jnp.minimum(
        jnp.sum((n_col * tend[None, :] <= s_eff[:, None]).astype(jnp.int32), axis=1), E - 1)
    q = s_eff - n_col * tstart[e_s]
    per = jnp.maximum(tiles[e_s], 1)
    wcol = jnp.clip(q // per, 0, n_col - 1)
    r = q - wcol * per
    used = s < n_used_steps
    first = jnp.logical_and(used, r == 0)
    spare = jnp.maximum(s - n_used_steps, 0)
    col = jnp.where(used, wcol, spare % n_col)
    tile = jnp.where(used, tstart[e_s] + r, tend[-1] + spare // n_col)
    i32 = lambda a: a.astype(jnp.int32)
    return i32(e_s), i32(wcol), i32(col), i32(tile), i32(used), i32(first)


def _expert_up_body(ex_ref, wcol_ref, col_ref, tile_ref, used_ref, first_ref, x_ref, w1g_ref,
                    w1l_ref, b1g_ref, b1l_ref, h_ref, wg_ref, wl_ref):
    del ex_ref, wcol_ref, col_ref, tile_ref
    s = pl.program_id(0)

    @pl.when(used_ref[s] == 0)
    def _():
        h_ref[...] = jnp.zeros(h_ref.shape, h_ref.dtype)

    @pl.when(first_ref[s] == 1)
    def _():
        wg_ref[...] = w1g_ref[...].astype(BF16)
        wl_ref[...] = w1l_ref[...].astype(BF16)

    @pl.when(used_ref[s] == 1)
    def _():
        lo, hi = _unpack_bf16_pairs(x_ref[...])
        x = jnp.concatenate([lo, hi], axis=1)
        glu = jnp.dot(x, wg_ref[...], preferred_element_type=F32) + b1g_ref[...]
        lin = jnp.dot(x, wl_ref[...], preferred_element_type=F32) + b1l_ref[...]
        glu = jnp.minimum(glu, SWIGLU_LIMIT)
        lin = jnp.clip(lin, -SWIGLU_LIMIT, SWIGLU_LIMIT)
        h_ref[...] = (glu * _sigmoid(SWIGLU_ALPHA * glu) * (lin + 1.0)).astype(h_ref.dtype)


def _expert_down_body(ex_ref, wcol_ref, col_ref, tile_ref, used_ref, first_ref, h_ref, w2_ref,
                      b2_ref, o_ref, wd_ref):
    del ex_ref, wcol_ref, col_ref, tile_ref
    s = pl.program_id(0)

    @pl.when(used_ref[s] == 0)
    def _():
        o_ref[...] = jnp.zeros(o_ref.shape, o_ref.dtype)

    @pl.when(first_ref[s] == 1)
    def _():
        wd_ref[...] = w2_ref[...].astype(BF16)

    @pl.when(used_ref[s] == 1)
    def _():
        y = jnp.dot(h_ref[...], wd_ref[...], preferred_element_type=F32) + b2_ref[...]
        o_ref[...] = _pack_bf16_pairs(y)


def _expert_ffn(tiles, xb, w1, b1, w2, b2, layer):
    P, Dh = xb.shape
    D = 2 * Dh
    _, E, _, F2 = w1.shape
    F = F2 // 2
    tm, tc = EXPERT_TILE_M, EXPERT_TILE_COLS
    max_tiles = P // tm
    b1r = b1.reshape(-1, E, 1, F2)
    b2r = b2.reshape(-1, E, 1, D)

    nf = F // tc
    steps = _expert_steps(tiles, nf, max_tiles)
    hidden = pl.pallas_call(
        _expert_up_body,
        out_shape=jax.ShapeDtypeStruct((P, F), BF16),
        grid_spec=pltpu.PrefetchScalarGridSpec(
            num_scalar_prefetch=6,
            grid=(nf * max_tiles,),
            in_specs=[
                pl.BlockSpec((tm, Dh), lambda s, ex, wc, col, tl, us, fr: (tl[s], 0)),
                pl.BlockSpec((None, None, D, tc),
                             lambda s, ex, wc, col, tl, us, fr: (layer, ex[s], 0, wc[s])),
                pl.BlockSpec((None, None, D, tc),
                             lambda s, ex, wc, col, tl, us, fr: (layer, ex[s], 0, nf + wc[s])),
                pl.BlockSpec((None, None, 1, tc),
                             lambda s, ex, wc, col, tl, us, fr: (layer, ex[s], 0, wc[s])),
                pl.BlockSpec((None, None, 1, tc),
                             lambda s, ex, wc, col, tl, us, fr: (layer, ex[s], 0, nf + wc[s])),
            ],
            out_specs=pl.BlockSpec((tm, tc), lambda s, ex, wc, col, tl, us, fr: (tl[s], col[s])),
            scratch_shapes=[pltpu.VMEM((D, tc), BF16), pltpu.VMEM((D, tc), BF16)],
        ),
        compiler_params=_params(1, EXPERT_VMEM_LIMIT_BYTES),
        name="moe_up",
    )(*steps, xb, w1, w1, b1r, b1r)

    steps = _expert_steps(tiles, 1, max_tiles)
    return pl.pallas_call(
        _expert_down_body,
        out_shape=jax.ShapeDtypeStruct((P, Dh), jnp.uint32),
        grid_spec=pltpu.PrefetchScalarGridSpec(
            num_scalar_prefetch=6,
            grid=(max_tiles,),
            in_specs=[
                pl.BlockSpec((tm, F), lambda s, ex, wc, col, tl, us, fr: (tl[s], 0)),
                pl.BlockSpec((None, None, F, D),
                             lambda s, ex, wc, col, tl, us, fr: (layer, ex[s], 0, 0)),
                pl.BlockSpec((None, None, 1, D),
                             lambda s, ex, wc, col, tl, us, fr: (layer, ex[s], 0, 0)),
            ],
            out_specs=pl.BlockSpec((tm, Dh), lambda s, ex, wc, col, tl, us, fr: (tl[s], 0)),
            scratch_shapes=[pltpu.VMEM((F, D), BF16)],
        ),
        compiler_params=_params(1, EXPERT_VMEM_LIMIT_BYTES),
        name="moe_down",
    )(*steps, hidden, w2, b2r)


def _combine_body(slot_ref, yb_ref, gate_ref, h_ref, o_ref, buf_ref, sems, *, tm, n_steps):
    i = pl.program_id(0)

    def start_gather(step, buf):
        base = step * (tm * TOP_K)

        def issue(r, carry):
            for k in range(TOP_K):
                s = slot_ref[base + r * TOP_K + k]
                pltpu.make_async_copy(yb_ref.at[pl.ds(s, 1), :],
                                      buf_ref.at[buf, k, pl.ds(r, 1), :], sems.at[buf]).start()
            return carry

        lax.fori_loop(0, tm, issue, 0, unroll=4)

    @pl.when(i == 0)
    def _():
        start_gather(0, 0)

    @pl.when(i + 1 < n_steps)
    def _():
        start_gather(i + 1, (i + 1) % 2)

    cur = i % 2
    for k in range(TOP_K):
        pltpu.make_async_copy(yb_ref.at[pl.ds(0, tm), :], buf_ref.at[cur, k], sems.at[cur]).wait()
    half = buf_ref.shape[-1]
    acc_lo = h_ref[:, :half]
    acc_hi = h_ref[:, half:]
    gates = gate_ref[...]
    for k in range(TOP_K):
        lo, hi = _unpack_bf16_pairs(buf_ref[cur, k], F32)
        acc_lo = acc_lo + gates[:, k:k + 1] * lo
        acc_hi = acc_hi + gates[:, k:k + 1] * hi
    o_ref[:, :half] = acc_lo
    o_ref[:, half:] = acc_hi


def _combine(slot_flat, yb, gates, h, *, tm=256):
    T, D = h.shape
    n_steps = T // tm
    return pl.pallas_call(
        functools.partial(_combine_body, tm=tm, n_steps=n_steps),
        out_shape=jax.ShapeDtypeStruct((T, D), F32),
        grid_spec=pltpu.PrefetchScalarGridSpec(
            num_scalar_prefetch=1,
            grid=(n_steps,),
            in_specs=[pl.BlockSpec(memory_space=pl.ANY),
                      pl.BlockSpec((tm, LANES), lambda i, s: (i, 0)),
                      pl.BlockSpec((tm, D), lambda i, s: (i, 0))],
            out_specs=pl.BlockSpec((tm, D), lambda i, s: (i, 0)),
            scratch_shapes=[pltpu.VMEM((2, TOP_K, tm, D // 2), jnp.uint32),
                            pltpu.SemaphoreType.DMA((2,))],
        ),
        compiler_params=_params(1),
        name="moe_combine",
    )(slot_flat, yb, gates, h)


def _moe(h, norm_g, router_w, router_b, w1, b1, w2, b2, layer):
    T, D = h.shape
    E, tm = N_EXPERTS, EXPERT_TILE_M
    xs, meta_i, meta_f, cnt = _router(h, norm_g, router_w, router_b)
    expert = meta_i[:, :TOP_K]
    rank = meta_i[:, TOP_K:2 * TOP_K]
    counts = cnt[0, :E]
    padded = (counts + tm - 1) // tm * tm
    pend = jnp.cumsum(padded)
    pstart = pend - padded
    slot_flat = (pstart[expert] + rank).reshape(-1).astype(jnp.int32)
    max_tiles = (T * TOP_K) // tm + E
    xb = _dispatch(slot_flat, xs, max_tiles * tm)
    yb = _expert_ffn(padded // tm, xb, w1, b1, w2, b2, layer)
    return _combine(slot_flat, yb, meta_f, h)


def _ple_body(*refs, final):
    (h_ref, p_ref, wp_ref, pn_ref, gn_ref, gw_ref, gb_ref) = refs[:7]
    fn_ref = refs[7] if final else None
    o_ref = refs[-1]
    h = h_ref[...]
    e = jnp.dot(p_ref[...].astype(BF16), wp_ref[...], preferred_element_type=F32)
    e = _rms(e, pn_ref[...])
    hn = _rms(h, gn_ref[...]).astype(BF16)
    gate = _sigmoid(jnp.dot(hn, gw_ref[...], preferred_element_type=F32) + gb_ref[...])
    out = h + gate * e
    if final:
        out = _rms(out, fn_ref[...])
    o_ref[...] = out


def _ple(h, p_i, w_p, p_norm, gate_norm, gate_w, gate_b, final_norm=None, *, tm=512):
    T, D = h.shape
    Pd = p_i.shape[1]
    final = final_norm is not None
    vec = pl.BlockSpec((1, D), lambda i: (0, 0))
    in_specs = [pl.BlockSpec((tm, D), lambda i: (i, 0)),
                pl.BlockSpec((tm, Pd), lambda i: (i, 0)),
                pl.BlockSpec((Pd, D), lambda i: (0, 0)),
                vec, vec,
                pl.BlockSpec((D, D), lambda i: (0, 0)),
                vec]
    args = [h, p_i, w_p.astype(BF16), p_norm.reshape(1, D), gate_norm.reshape(1, D),
            gate_w.astype(BF16), gate_b.reshape(1, D)]
    if final:
        in_specs.append(vec)
        args.append(final_norm.reshape(1, D))
    return pl.pallas_call(
        functools.partial(_ple_body, final=final),
        out_shape=jax.ShapeDtypeStruct((T, D), F32),
        grid=(T // tm,),
        in_specs=in_specs,
        out_specs=pl.BlockSpec((tm, D), lambda i: (i, 0)),
        compiler_params=_params(1),
        name="ple",
    )(*args)


def kernel(x, p, a_norm, a_w_in, a_ln_g, a_ln_b, a_w_s, a_b_s, a_w_out, b_norm, b_w_qkv, b_lq1, b_lk1, b_lq2, b_lk2, b_subln, b_w_out, moe_norm, router_w, router_b, moe_w1, moe_b1, moe_w2, moe_b2, ple_w, ple_norm, ple_gate_norm, ple_gate_w, ple_gate_b, final_norm):
    B, S, D = x.shape
    assert B == 1, "attention and chunked mixing treat the row axis as one sequence"
    depth = p.shape[0]
    h = x.reshape(B * S, D)
    for i in range(depth):
        j = i // 2
        if i % 2 == 0:
            z = _norm_matmul(h, a_norm[j], a_w_in[j].astype(BF16), act="gelu",
                             out_dtype=BF16, name="gmlp_in")
            h = _sgu(z, a_ln_g[j], a_ln_b[j], a_w_s[j], a_b_s[j], a_w_out[j].astype(BF16), h)
        else:
            qkv = _qkv_rope(h, b_norm[j], b_w_qkv[j].astype(BF16))
            o = _diff_attn(qkv, b_lq1[j], b_lk1[j], b_lq2[j], b_lk2[j], b_subln[j], i)
            h = _norm_matmul(o, None, b_w_out[j].astype(BF16), residual=h, name="attn_out")
        h = _moe(h, moe_norm[i], router_w[i], router_b[i], moe_w1, moe_b1, moe_w2, moe_b2, i)
        h = _ple(h, p[i].reshape(B * S, -1), ple_w[i], ple_norm[i], ple_gate_norm[i],
                 ple_gate_w[i], ple_gate_b[i],
                 final_norm if i == depth - 1 else None)
    return h.reshape(B, S, D)
```

```python
import functools
import math

import jax
import jax.numpy as jnp
from jax import lax
from jax.experimental import pallas as pl
from jax.experimental.pallas import tpu as pltpu

F32 = jnp.float32
BF16 = jnp.bfloat16

RMS_EPS = 1e-6
LN_EPS = 1e-5
CHUNK = 128
GMLP_GROUPS = 16
DIFF_HEAD_DIM = 128
DIFF_V_DIM = 256
ROPE_THETA = 500000.0
ROPE_DIM = DIFF_HEAD_DIM // 4
ROPE_HALF = ROPE_DIM // 2
N_EXPERTS = 32
TOP_K = 4
SWIGLU_ALPHA = 1.702
SWIGLU_LIMIT = 7.0
Q_SCALE = DIFF_HEAD_DIM ** -0.5 * math.log2(math.e)

LANES = 128
VMEM_LIMIT_BYTES = 48 * 1024 * 1024

EXPERT_TILE_M = 512
EXPERT_TILE_COLS = 1024
EXPERT_VMEM_LIMIT_BYTES = 56 * 1024 * 1024


def _params(n_axes, vmem=VMEM_LIMIT_BYTES):
    return pltpu.CompilerParams(
        dimension_semantics=("arbitrary",) * n_axes, vmem_limit_bytes=vmem)


def _rms(x, g):
    var = jnp.mean(x * x, axis=-1, keepdims=True)
    return x * lax.rsqrt(var + RMS_EPS) * g


def _sigmoid(x):
    return 1.0 / (1.0 + jnp.exp(-x))


def _pack_bf16_pairs(x):
    n = x.shape[1] // 2
    bits = lax.bitcast_convert_type(x.astype(BF16).astype(F32), jnp.uint32)
    return (bits[:, :n] >> 16) | bits[:, n:]


def _unpack_bf16_pairs(packed, dtype=BF16):
    lo = lax.bitcast_convert_type(packed << 16, F32)
    hi = lax.bitcast_convert_type(packed & jnp.uint32(0xFFFF0000), F32)
    return lo.astype(dtype), hi.astype(dtype)


def _norm_matmul_body(*refs, norm, has_bias, act, has_res):
    it = iter(refs)
    x_ref = next(it)
    g_ref = next(it) if norm else None
    w_ref = next(it)
    b_ref = next(it) if has_bias else None
    r_ref = next(it) if has_res else None
    o_ref = next(it)
    xn_ref = next(it)

    @pl.when(pl.program_id(1) == 0)
    def _():
        x = x_ref[...].astype(F32)
        if norm:
            x = _rms(x, g_ref[...])
        xn_ref[...] = x.astype(BF16)

    acc = jnp.dot(xn_ref[...], w_ref[...], preferred_element_type=F32)
    if has_bias:
        acc = acc + b_ref[...]
    if act == "gelu":
        acc = 0.5 * acc * (1.0 + lax.erf(acc * (1.0 / math.sqrt(2.0))))
    if has_res:
        acc = acc + r_ref[...]
    o_ref[...] = acc.astype(o_ref.dtype)


def _norm_matmul(x, g, w, *, bias=None, act=None, residual=None, out_dtype=F32,
                 tm=1024, tn=512, name="norm_matmul"):
    M, K = x.shape
    N = w.shape[1]
    norm = g is not None
    in_specs = [pl.BlockSpec((tm, K), lambda i, j: (i, 0))]
    args = [x]
    if norm:
        in_specs.append(pl.BlockSpec((1, K), lambda i, j: (0, 0)))
        args.append(g.reshape(1, K))
    in_specs.append(pl.BlockSpec((K, tn), lambda i, j: (0, j)))
    args.append(w)
    if bias is not None:
        in_specs.append(pl.BlockSpec((1, tn), lambda i, j: (0, j)))
        args.append(bias.reshape(1, N))
    if residual is not None:
        in_specs.append(pl.BlockSpec((tm, tn), lambda i, j: (i, j)))
        args.append(residual)
    body = functools.partial(_norm_matmul_body, norm=norm, has_bias=bias is not None,
                             act=act, has_res=residual is not None)
    return pl.pallas_call(
        body,
        out_shape=jax.ShapeDtypeStruct((M, N), out_dtype),
        grid=(M // tm, N // tn),
        in_specs=in_specs,
        out_specs=pl.BlockSpec((tm, tn), lambda i, j: (i, j)),
        scratch_shapes=[pltpu.VMEM((tm, K), BF16)],
        compiler_params=_params(2),
        name=name,
    )(*args)


def _sgu_body(u_ref, v_ref, lng_ref, lnb_ref, ws_ref, bias_ref, wo_ref, h_ref, o_ref,
              wt_ref, y_ref, *, tm):
    @pl.when(pl.program_id(0) == 0)
    def _():
        row = lax.broadcasted_iota(jnp.int32, (CHUNK, CHUNK), 0)
        col = lax.broadcasted_iota(jnp.int32, (CHUNK, CHUNK), 1)
        causal = col <= row
        for gi in range(GMLP_GROUPS):
            wt_ref[gi] = jnp.where(causal, ws_ref[gi], 0.0).astype(BF16)

    v = v_ref[...].astype(F32)
    mu = jnp.mean(v, axis=-1, keepdims=True)
    vc = v - mu
    var = jnp.mean(vc * vc, axis=-1, keepdims=True)
    vn = (vc * lax.rsqrt(var + LN_EPS) * lng_ref[...] + lnb_ref[...]).astype(BF16)
    for c in range(tm // CHUNK):
        rows = slice(c * CHUNK, (c + 1) * CHUNK)
        for gi in range(GMLP_GROUPS):
            cols = slice(gi * LANES, (gi + 1) * LANES)
            sv = jnp.dot(wt_ref[gi], vn[rows, cols], preferred_element_type=F32)
            sv = sv + bias_ref[:, cols]
            y_ref[rows, cols] = (u_ref[rows, cols].astype(F32) * sv).astype(BF16)
    o_ref[...] = h_ref[...] + jnp.dot(y_ref[...], wo_ref[...], preferred_element_type=F32)


def _sgu(z, ln_g, ln_b, w_s, b_s, w_out, h, *, tm=256):
    T, D = h.shape
    W = z.shape[1] // 2
    bias_tile = jnp.repeat(b_s.T, W // GMLP_GROUPS, axis=1)
    return pl.pallas_call(
        functools.partial(_sgu_body, tm=tm),
        out_shape=jax.ShapeDtypeStruct((T, D), F32),
        grid=(T // tm,),
        in_specs=[
            pl.BlockSpec((tm, W), lambda i: (i, 0)),
            pl.BlockSpec((tm, W), lambda i: (i, 1)),
            pl.BlockSpec((1, W), lambda i: (0, 0)),
            pl.BlockSpec((1, W), lambda i: (0, 0)),
            pl.BlockSpec((GMLP_GROUPS, CHUNK, CHUNK), lambda i: (0, 0, 0)),
            pl.BlockSpec((CHUNK, W), lambda i: (0, 0)),
            pl.BlockSpec((W, D), lambda i: (0, 0)),
            pl.BlockSpec((tm, D), lambda i: (i, 0)),
        ],
        out_specs=pl.BlockSpec((tm, D), lambda i: (i, 0)),
        scratch_shapes=[pltpu.VMEM((GMLP_GROUPS, CHUNK, CHUNK), BF16),
                        pltpu.VMEM((tm, W), BF16)],
        compiler_params=_params(1),
        name="sgu",
    )(z, z, ln_g.reshape(1, W), ln_b.reshape(1, W), w_s, bias_tile, w_out, h)


def _qkv_body(x_ref, g_ref, w_ref, c_ref, s1_ref, s2_ref, o_ref, xn_ref, *, heads_per_tile):
    @pl.when(pl.program_id(1) == 0)
    def _():
        xn_ref[...] = _rms(x_ref[...], g_ref[...]).astype(BF16)

    cosf, s1, s2 = c_ref[...], s1_ref[...], s2_ref[...]
    xn = xn_ref[...]
    for grp in range(heads_per_tile // 2):
        acc = jnp.dot(xn, w_ref[:, grp * 2 * LANES:(grp + 1) * 2 * LANES],
                      preferred_element_type=F32)
        for hh in range(2):
            seg = acc[:, hh * LANES:(hh + 1) * LANES]
            up = pltpu.roll(seg, LANES - ROPE_HALF, 1)
            dn = pltpu.roll(seg, ROPE_HALF, 1)
            cols = slice((2 * grp + hh) * LANES, (2 * grp + hh + 1) * LANES)
            o_ref[:, cols] = (seg * cosf + up * s1 + dn * s2).astype(o_ref.dtype)


def _rope_tables(T):
    inv_freq = jnp.power(ROPE_THETA, -jnp.arange(0, ROPE_DIM, 2, dtype=F32) / ROPE_DIM)
    ang = jnp.arange(T, dtype=F32)[:, None] * inv_freq[None, :]
    cos, sin = jnp.cos(ang), jnp.sin(ang)
    zeros = jnp.zeros((T, LANES - ROPE_DIM), F32)
    half0 = jnp.zeros((T, ROPE_HALF), F32)
    c_tab = jnp.concatenate([cos, cos, jnp.ones((T, LANES - ROPE_DIM), F32)], axis=1)
    s1_tab = jnp.concatenate([-sin, half0, zeros], axis=1)
    s2_tab = jnp.concatenate([half0, sin, zeros], axis=1)
    none = jnp.zeros((T, LANES), F32)
    return (jnp.stack([c_tab * Q_SCALE, c_tab, jnp.ones((T, LANES), F32)]),
            jnp.stack([s1_tab * Q_SCALE, s1_tab, none]),
            jnp.stack([s2_tab * Q_SCALE, s2_tab, none]))


def _qkv_rope(h, g, w, *, tm=1024, tn=512):
    T, D = h.shape
    N = w.shape[1]
    c_tab, s1_tab, s2_tab = _rope_tables(T)
    n_q, n_qk = D // tn, 2 * D // tn

    def kind(j):
        return jnp.where(j < n_q, 0, jnp.where(j < n_qk, 1, 2))

    tab_spec = pl.BlockSpec((None, tm, LANES), lambda i, j: (kind(j), i, 0))
    body = functools.partial(_qkv_body, heads_per_tile=tn // LANES)
    return pl.pallas_call(
        body,
        out_shape=jax.ShapeDtypeStruct((T, N), BF16),
        grid=(T // tm, N // tn),
        in_specs=[pl.BlockSpec((tm, D), lambda i, j: (i, 0)),
                  pl.BlockSpec((1, D), lambda i, j: (0, 0)),
                  pl.BlockSpec((D, tn), lambda i, j: (0, j)),
                  tab_spec, tab_spec, tab_spec],
        out_specs=pl.BlockSpec((tm, tn), lambda i, j: (i, j)),
        scratch_shapes=[pltpu.VMEM((tm, D), BF16)],
        compiler_params=_params(2),
        name="qkv_rope",
    )(h, g.reshape(1, D), w, c_tab, s1_tab, s2_tab)


def _diff_attn_body(lq1_ref, lk1_ref, lq2_ref, lk2_ref, g_ref, q_ref, k_ref, v_ref, o_ref,
                    m0_ref, l0_ref, acc0_ref, m1_ref, l1_ref, acc1_ref, *, tq, lambda_init):
    qi = pl.program_id(1)
    Dh = DIFF_HEAD_DIM
    state = ((m0_ref, l0_ref, acc0_ref), (m1_ref, l1_ref, acc1_ref))
    for m_ref, l_ref, acc_ref in state:
        m_ref[...] = jnp.full(m_ref.shape, -jnp.inf, F32)
        l_ref[...] = jnp.zeros(l_ref.shape, F32)
        acc_ref[...] = jnp.zeros(acc_ref.shape, F32)

    n_rep = tq // LANES

    def chunk(j, masked):
        start = pl.multiple_of(j * tq, tq)
        k = k_ref[pl.ds(start, tq), :]
        v = v_ref[pl.ds(start, tq), :]
        scores = []
        for c in range(2):
            s = lax.dot_general(q_ref[:, c * Dh:(c + 1) * Dh], k[:, c * Dh:(c + 1) * Dh],
                                (((1,), (1,)), ((), ())), preferred_element_type=F32)
            if masked:
                row = lax.broadcasted_iota(jnp.int32, (tq, tq), 0)
                col = lax.broadcasted_iota(jnp.int32, (tq, tq), 1)
                s = jnp.where(col <= row, s, -jnp.inf)
            scores.append(s)
        for s, (m_ref, l_ref, acc_ref) in zip(scores, state):
            m_prev = m_ref[...]
            m_new = jnp.maximum(m_prev, jnp.max(s, axis=-1, keepdims=True))
            p = jnp.exp2(s - jnp.concatenate([m_new] * n_rep, axis=1))
            alpha = jnp.exp2(m_prev - m_new)
            psum = p[:, :LANES]
            for r in range(1, n_rep):
                psum = psum + p[:, r * LANES:(r + 1) * LANES]
            l_ref[...] = alpha * l_ref[...] + psum
            acc_ref[...] = (jnp.concatenate([alpha] * (DIFF_V_DIM // LANES), axis=1) * acc_ref[...]
                            + jnp.dot(p.astype(BF16), v, preferred_element_type=F32))
            m_ref[...] = m_new

    def full_chunk(j, carry):
        chunk(j, False)
        return carry

    lax.fori_loop(0, qi, full_chunk, 0)
    chunk(qi, True)

    lam = (jnp.exp(jnp.sum(lq1_ref[...] * lk1_ref[...], axis=-1, keepdims=True))
           - jnp.exp(jnp.sum(lq2_ref[...] * lk2_ref[...], axis=-1, keepdims=True))
           + lambda_init)
    l0 = jnp.sum(l0_ref[...], axis=-1, keepdims=True)
    l1 = jnp.sum(l1_ref[...], axis=-1, keepdims=True)
    o = acc0_ref[...] / l0 - lam * (acc1_ref[...] / l1)
    o_ref[...] = (_rms(o, g_ref[...]) * (1.0 - lambda_init)).astype(o_ref.dtype)


def _diff_attn(qkv, lq1, lk1, lq2, lk2, subln_g, layer_idx, *, tq=1024):
    T = qkv.shape[0]
    Dv = DIFF_V_DIM
    D = qkv.shape[1] // 3
    H = D // Dv
    lambda_init = 0.8 - 0.6 * math.exp(-0.3 * layer_idx)
    vec = pl.BlockSpec((1, DIFF_HEAD_DIM), lambda h, i: (0, 0))
    body = functools.partial(_diff_attn_body, tq=tq, lambda_init=lambda_init)
    return pl.pallas_call(
        body,
        out_shape=jax.ShapeDtypeStruct((T, D), BF16),
        grid=(H, T // tq),
        in_specs=[vec, vec, vec, vec,
                  pl.BlockSpec((1, Dv), lambda h, i: (0, 0)),
                  pl.BlockSpec((tq, Dv), lambda h, i: (i, h)),
                  pl.BlockSpec((T, Dv), lambda h, i: (0, H + h),
                               pipeline_mode=pl.Buffered(1)),
                  pl.BlockSpec((T, Dv), lambda h, i: (0, 2 * H + h),
                               pipeline_mode=pl.Buffered(1))],
        out_specs=pl.BlockSpec((tq, Dv), lambda h, i: (i, h)),
        scratch_shapes=[pltpu.VMEM((tq, LANES), F32), pltpu.VMEM((tq, LANES), F32),
                        pltpu.VMEM((tq, Dv), F32)] * 2,
        compiler_params=_params(2),
        name="diff_attn",
    )(lq1.reshape(1, -1), lk1.reshape(1, -1), lq2.reshape(1, -1), lk2.reshape(1, -1),
      subln_g.reshape(1, Dv), qkv, qkv, qkv)


def _router_body(h_ref, g_ref, rw_ref, rb_ref, xs_ref, mi_ref, mf_ref, cnt_ref, run_ref, *, tm):
    @pl.when(pl.program_id(0) == 0)
    def _():
        run_ref[...] = jnp.zeros(run_ref.shape, F32)

    xs = _rms(h_ref[...], g_ref[...])
    xs_ref[...] = _pack_bf16_pairs(xs)
    xs_hi = xs.astype(BF16)
    xs_lo = (xs - xs_hi.astype(F32)).astype(BF16)
    rw_hi = rw_ref[...].astype(BF16)
    rw_lo = (rw_ref[...] - rw_hi.astype(F32)).astype(BF16)
    logits = (jnp.dot(xs_hi, rw_hi, preferred_element_type=F32)
              + jnp.dot(xs_hi, rw_lo, preferred_element_type=F32)
              + jnp.dot(xs_lo, rw_hi, preferred_element_type=F32)) + rb_ref[...]
    lane = lax.broadcasted_iota(jnp.int32, (tm, LANES), 1)
    work = jnp.where(lane < N_EXPERTS, logits, -jnp.inf)
    vals, idxs, hots = [], [], []
    for _ in range(TOP_K):
        mx = jnp.max(work, axis=-1, keepdims=True)
        idx = jnp.min(jnp.where(work == mx, lane, LANES), axis=-1, keepdims=True)
        hot = lane == idx
        vals.append(mx)
        idxs.append(idx)
        hots.append(hot)
        work = jnp.where(hot, -jnp.inf, work)
    exps = [jnp.exp(v - vals[0]) for v in vals]
    denom = exps[0] + exps[1] + exps[2] + exps[3]

    sel = jnp.zeros((tm, LANES), F32)
    for hot in hots:
        sel = sel + hot.astype(F32)
    row = lax.broadcasted_iota(jnp.int32, (tm, tm), 0)
    col = lax.broadcasted_iota(jnp.int32, (tm, tm), 1)
    before = (col < row).astype(BF16)
    rank_all = jnp.dot(before, sel.astype(BF16), preferred_element_type=F32) + run_ref[...]
    run_ref[...] = run_ref[...] + jnp.sum(sel, axis=0, keepdims=True)
    cnt_ref[...] = run_ref[...].astype(jnp.int32)

    mi = jnp.zeros((tm, LANES), jnp.int32)
    mf = jnp.zeros((tm, LANES), F32)
    for k in range(TOP_K):
        rank_k = jnp.sum(jnp.where(hots[k], rank_all, 0.0), axis=-1, keepdims=True)
        mi = jnp.where(lane == k, idxs[k], mi)
        mi = jnp.where(lane == TOP_K + k, rank_k.astype(jnp.int32), mi)
        mf = jnp.where(lane == k, exps[k] / denom, mf)
    mi_ref[...] = mi
    mf_ref[...] = mf


def _router(h, g, rw, rb, *, tm=512):
    T, D = h.shape
    rw_pad = jnp.zeros((D, LANES), F32).at[:, :N_EXPERTS].set(rw)
    rb_pad = jnp.zeros((1, LANES), F32).at[0, :N_EXPERTS].set(rb)
    return pl.pallas_call(
        functools.partial(_router_body, tm=tm),
        out_shape=(jax.ShapeDtypeStruct((T, D // 2), jnp.uint32),
                   jax.ShapeDtypeStruct((T, LANES), jnp.int32),
                   jax.ShapeDtypeStruct((T, LANES), F32),
                   jax.ShapeDtypeStruct((1, LANES), jnp.int32)),
        grid=(T // tm,),
        in_specs=[pl.BlockSpec((tm, D), lambda i: (i, 0)),
                  pl.BlockSpec((1, D), lambda i: (0, 0)),
                  pl.BlockSpec((D, LANES), lambda i: (0, 0)),
                  pl.BlockSpec((1, LANES), lambda i: (0, 0))],
        out_specs=(pl.BlockSpec((tm, D // 2), lambda i: (i, 0)),
                   pl.BlockSpec((tm, LANES), lambda i: (i, 0)),
                   pl.BlockSpec((tm, LANES), lambda i: (i, 0)),
                   pl.BlockSpec((1, LANES), lambda i: (0, 0))),
        scratch_shapes=[pltpu.VMEM((1, LANES), F32)],
        compiler_params=_params(1),
        name="router",
    )(h, g.reshape(1, D), rw_pad, rb_pad)


def _dispatch_body(slot_ref, xs_ref, xb_in_ref, xb_ref, sem, *, tm):
    del xb_in_ref
    base = pl.program_id(0) * (tm * TOP_K)

    def row_copy(r, s):
        return pltpu.make_async_copy(xs_ref.at[pl.ds(r, 1), :], xb_ref.at[pl.ds(s, 1), :], sem)

    def issue(r, carry):
        for k in range(TOP_K):
            row_copy(r, slot_ref[base + r * TOP_K + k]).start()
        return carry

    lax.fori_loop(0, tm, issue, 0, unroll=8)
    for _ in range(TOP_K):
        pltpu.make_async_copy(xs_ref, xb_ref.at[pl.ds(0, tm), :], sem).wait()


def _dispatch(slot_flat, xs, n_rows, *, tm=512):
    T, D = xs.shape
    xb_init = jnp.zeros((n_rows, D), xs.dtype)
    return pl.pallas_call(
        functools.partial(_dispatch_body, tm=tm),
        out_shape=jax.ShapeDtypeStruct((n_rows, D), xs.dtype),
        grid_spec=pltpu.PrefetchScalarGridSpec(
            num_scalar_prefetch=1,
            grid=(T // tm,),
            in_specs=[pl.BlockSpec((tm, D), lambda i, s: (i, 0)),
                      pl.BlockSpec(memory_space=pl.ANY)],
            out_specs=pl.BlockSpec(memory_space=pl.ANY),
            scratch_shapes=[pltpu.SemaphoreType.DMA],
        ),
        input_output_aliases={2: 0},
        compiler_params=_params(1),
        name="moe_dispatch",
    )(slot_flat, xs, xb_init)


def _expert_steps(tiles, n_col, max_tiles):
    E = tiles.shape[0]
    tend = jnp.cumsum(tiles)
    tstart = tend - tiles
    n_used_steps = n_col * tend[-1]
    s = jnp.arange(n_col * max_tiles, dtype=jnp.int32)
    s_eff = jnp.minimum(s, jnp.maximum(n_used_steps - 1, 0))
    e_s = jnp.minimum(
        jnp.sum((n_col * tend[None, :] <= s_eff[:, None]).astype(jnp.int32), axis=1), E - 1)
    q = s_eff - n_col * tstart[e_s]
    per = jnp.maximum(tiles[e_s], 1)
    wcol = jnp.clip(q // per, 0, n_col - 1)
    r = q - wcol * per
    used = s < n_used_steps
    first = jnp.logical_and(used, r == 0)
    spare = jnp.maximum(s - n_used_steps, 0)
    col = jnp.where(used, wcol, spare % n_col)
    tile = jnp.where(used, tstart[e_s] + r, tend[-1] + spare // n_col)
    i32 = lambda a: a.astype(jnp.int32)
    return i32(e_s), i32(wcol), i32(col), i32(tile), i32(used), i32(first)


def _expert_up_body(ex_ref, wcol_ref, col_ref, tile_ref, used_ref, first_ref, x_ref, w1g_ref,
                    w1l_ref, b1g_ref, b1l_ref, h_ref, wg_ref, wl_ref):
    del ex_ref, wcol_ref, col_ref, tile_ref
    s = pl.program_id(0)

    @pl.when(used_ref[s] == 0)
    def _():
        h_ref[...] = jnp.zeros(h_ref.shape, h_ref.dtype)

    @pl.when(first_ref[s] == 1)
    def _():
        wg_ref[...] = w1g_ref[...].astype(BF16)
        wl_ref[...] = w1l_ref[...].astype(BF16)

    @pl.when(used_ref[s] == 1)
    def _():
        lo, hi = _unpack_bf16_pairs(x_ref[...])
        x = jnp.concatenate([lo, hi], axis=1)
        glu = jnp.dot(x, wg_ref[...], preferred_element_type=F32) + b1g_ref[...]
        lin = jnp.dot(x, wl_ref[...], preferred_element_type=F32) + b1l_ref[...]
        glu = jnp.minimum(glu, SWIGLU_LIMIT)
        lin = jnp.clip(lin, -SWIGLU_LIMIT, SWIGLU_LIMIT)
        h_ref[...] = (glu * _sigmoid(SWIGLU_ALPHA * glu) * (lin + 1.0)).astype(h_ref.dtype)


def _expert_down_body(ex_ref, wcol_ref, col_ref, tile_ref, used_ref, first_ref, h_ref, w2_ref,
                      b2_ref, o_ref, wd_ref):
    del ex_ref, wcol_ref, col_ref, tile_ref
    s = pl.program_id(0)

    @pl.when(used_ref[s] == 0)
    def _():
        o_ref[...] = jnp.zeros(o_ref.shape, o_ref.dtype)

    @pl.when(first_ref[s] == 1)
    def _():
        wd_ref[...] = w2_ref[...].astype(BF16)

    @pl.when(used_ref[s] == 1)
    def _():
        y = jnp.dot(h_ref[...], wd_ref[...], preferred_element_type=F32) + b2_ref[...]
        o_ref[...] = _pack_bf16_pairs(y)


def _expert_ffn(tiles, xb, w1, b1, w2, b2, layer):
    P, Dh = xb.shape
    D = 2 * Dh
    _, E, _, F2 = w1.shape
    F = F2 // 2
    tm, tc = EXPERT_TILE_M, EXPERT_TILE_COLS
    max_tiles = P // tm
    b1r = b1.reshape(-1, E, 1, F2)
    b2r = b2.reshape(-1, E, 1, D)

    nf = F // tc
    steps = _expert_steps(tiles, nf, max_tiles)
    hidden = pl.pallas_call(
        _expert_up_body,
        out_shape=jax.ShapeDtypeStruct((P, F), BF16),
        grid_spec=pltpu.PrefetchScalarGridSpec(
            num_scalar_prefetch=6,
            grid=(nf * max_tiles,),
            in_specs=[
                pl.BlockSpec((tm, Dh), lambda s, ex, wc, col, tl, us, fr: (tl[s], 0)),
                pl.BlockSpec((None, None, D, tc),
                             lambda s, ex, wc, col, tl, us, fr: (layer, ex[s], 0, wc[s])),
                pl.BlockSpec((None, None, D, tc),
                             lambda s, ex, wc, col, tl, us, fr: (layer, ex[s], 0, nf + wc[s])),
                pl.BlockSpec((None, None, 1, tc),
                             lambda s, ex, wc, col, tl, us, fr: (layer, ex[s], 0, wc[s])),
                pl.BlockSpec((None, None, 1, tc),
                             lambda s, ex, wc, col, tl, us, fr: (layer, ex[s], 0, nf + wc[s])),
            ],
            out_specs=pl.BlockSpec((tm, tc), lambda s, ex, wc, col, tl, us, fr: (tl[s], col[s])),
            scratch_shapes=[pltpu.VMEM((D, tc), BF16), pltpu.VMEM((D, tc), BF16)],
        ),
        compiler_params=_params(1, EXPERT_VMEM_LIMIT_BYTES),
        name="moe_up",
    )(*steps, xb, w1, w1, b1r, b1r)

    steps = _expert_steps(tiles, 1, max_tiles)
    return pl.pallas_call(
        _expert_down_body,
        out_shape=jax.ShapeDtypeStruct((P, Dh), jnp.uint32),
        grid_spec=pltpu.PrefetchScalarGridSpec(
            num_scalar_prefetch=6,
            grid=(max_tiles,),
            in_specs=[
                pl.BlockSpec((tm, F), lambda s, ex, wc, col, tl, us, fr: (tl[s], 0)),
                pl.BlockSpec((None, None, F, D),
                             lambda s, ex, wc, col, tl, us, fr: (layer, ex[s], 0, 0)),
                pl.BlockSpec((None, None, 1, D),
                             lambda s, ex, wc, col, tl, us, fr: (layer, ex[s], 0, 0)),
            ],
            out_specs=pl.BlockSpec((tm, Dh), lambda s, ex, wc, col, tl, us, fr: (tl[s], 0)),
            scratch_shapes=[pltpu.VMEM((F, D), BF16)],
        ),
        compiler_params=_params(1, EXPERT_VMEM_LIMIT_BYTES),
        name="moe_down",
    )(*steps, hidden, w2, b2r)


def _combine_body(slot_ref, yb_ref, gate_ref, h_ref, o_ref, buf_ref, sems, *, tm, n_steps):
    i = pl.program_id(0)

    def start_gather(step, buf):
        base = step * (tm * TOP_K)

        def issue(r, carry):
            for k in range(TOP_K):
                s = slot_ref[base + r * TOP_K + k]
                pltpu.make_async_copy(yb_ref.at[pl.ds(s, 1), :],
                                      buf_ref.at[buf, k, pl.ds(r, 1), :], sems.at[buf]).start()
            return carry

        lax.fori_loop(0, tm, issue, 0, unroll=8)

    @pl.when(i == 0)
    def _():
        start_gather(0, 0)

    for buf in range(2):
        @pl.when(jnp.logical_and(i + 1 < n_steps, (i + 1) % 2 == buf))
        def _():
            start_gather(i + 1, buf)

    cur = i % 2
    for k in range(TOP_K):
        pltpu.make_async_copy(yb_ref.at[pl.ds(0, tm), :], buf_ref.at[cur, k], sems.at[cur]).wait()
    half = buf_ref.shape[-1]
    acc_lo = h_ref[:, :half]
    acc_hi = h_ref[:, half:]
    gates = gate_ref[...]
    for k in range(TOP_K):
        lo, hi = _unpack_bf16_pairs(buf_ref[cur, k], F32)
        acc_lo = acc_lo + gates[:, k:k + 1] * lo
        acc_hi = acc_hi + gates[:, k:k + 1] * hi
    o_ref[:, :half] = acc_lo
    o_ref[:, half:] = acc_hi


def _combine(slot_flat, yb, gates, h, *, tm=256):
    T, D = h.shape
    n_steps = T // tm
    return pl.pallas_call(
        functools.partial(_combine_body, tm=tm, n_steps=n_steps),
        out_shape=jax.ShapeDtypeStruct((T, D), F32),
        grid_spec=pltpu.PrefetchScalarGridSpec(
            num_scalar_prefetch=1,
            grid=(n_steps,),
            in_specs=[pl.BlockSpec(memory_space=pl.ANY),
                      pl.BlockSpec((tm, LANES), lambda i, s: (i, 0)),
                      pl.BlockSpec((tm, D), lambda i, s: (i, 0))],
            out_specs=pl.BlockSpec((tm, D), lambda i, s: (i, 0)),
            scratch_shapes=[pltpu.VMEM((2, TOP_K, tm, D // 2), jnp.uint32),
                            pltpu.SemaphoreType.DMA((2,))],
        ),
        compiler_params=_params(1),
        name="moe_combine",
    )(slot_flat, yb, gates, h)


def _moe(h, norm_g, router_w, router_b, w1, b1, w2, b2, layer):
    T, D = h.shape
    E, tm = N_EXPERTS, EXPERT_TILE_M
    xs, meta_i, meta_f, cnt = _router(h, norm_g, router_w, router_b)
    expert = meta_i[:, :TOP_K]
    rank = meta_i[:, TOP_K:2 * TOP_K]
    counts = cnt[0, :E]
    padded = (counts + tm - 1) // tm * tm
    pend = jnp.cumsum(padded)
    pstart = pend - padded
    slot_flat = (pstart[expert] + rank).reshape(-1).astype(jnp.int32)
    max_tiles = (T * TOP_K) // tm + E
    xb = _dispatch(slot_flat, xs, max_tiles * tm)
    yb = _expert_ffn(padded // tm, xb, w1, b1, w2, b2, layer)
    return _combine(slot_flat, yb, meta_f, h)


def _ple_body(*refs, final):
    (h_ref, p_ref, wp_ref, pn_ref, gn_ref, gw_ref, gb_ref) = refs[:7]
    fn_ref = refs[7] if final else None
    o_ref = refs[-1]
    h = h_ref[...]
    e = jnp.dot(p_ref[...].astype(BF16), wp_ref[...], preferred_element_type=F32)
    e = _rms(e, pn_ref[...])
    hn = _rms(h, gn_ref[...]).astype(BF16)
    gate = _sigmoid(jnp.dot(hn, gw_ref[...], preferred_element_type=F32) + gb_ref[...])
    out = h + gate * e
    if final:
        out = _rms(out, fn_ref[...])
    o_ref[...] = out


def _ple(h, p_i, w_p, p_norm, gate_norm, gate_w, gate_b, final_norm=None, *, tm=512):
    T, D = h.shape
    Pd = p_i.shape[1]
    final = final_norm is not None
    vec = pl.BlockSpec((1, D), lambda i: (0, 0))
    in_specs = [pl.BlockSpec((tm, D), lambda i: (i, 0)),
                pl.BlockSpec((tm, Pd), lambda i: (i, 0)),
                pl.BlockSpec((Pd, D), lambda i: (0, 0)),
                vec, vec,
                pl.BlockSpec((D, D), lambda i: (0, 0)),
                vec]
    args = [h, p_i, w_p.astype(BF16), p_norm.reshape(1, D), gate_norm.reshape(1, D),
            gate_w.astype(BF16), gate_b.reshape(1, D)]
    if final:
        in_specs.append(vec)
        args.append(final_norm.reshape(1, D))
    return pl.pallas_call(
        functools.partial(_ple_body, final=final),
        out_shape=jax.ShapeDtypeStruct((T, D), F32),
        grid=(T // tm,),
        in_specs=in_specs,
        out_specs=pl.BlockSpec((tm, D), lambda i: (i, 0)),
        compiler_params=_params(1),
        name="ple",
    )(*args)


def kernel(x, p, a_norm, a_w_in, a_ln_g, a_ln_b, a_w_s, a_b_s, a_w_out, b_norm, b_w_qkv, b_lq1, b_lk1, b_lq2, b_lk2, b_subln, b_w_out, moe_norm, router_w, router_b, moe_w1, moe_b1, moe_w2, moe_b2, ple_w, ple_norm, ple_gate_norm, ple_gate_w, ple_gate_b, final_norm):
    B, S, D = x.shape
    assert B == 1, "attention and chunked mixing treat the row axis as one sequence"
    depth = p.shape[0]
    h = x.reshape(B * S, D)
    for i in range(depth):
        j = i // 2
        if i % 2 == 0:
            z = _norm_matmul(h, a_norm[j], a_w_in[j].astype(BF16), act="gelu",
                             out_dtype=BF16, name="gmlp_in")
            h = _sgu(z, a_ln_g[j], a_ln_b[j], a_w_s[j], a_b_s[j], a_w_out[j].astype(BF16), h)
        else:
            qkv = _qkv_rope(h, b_norm[j], b_w_qkv[j].astype(BF16))
            o = _diff_attn(qkv, b_lq1[j], b_lk1[j], b_lq2[j], b_lk2[j], b_subln[j], i)
            h = _norm_matmul(o, None, b_w_out[j].astype(BF16), residual=h, name="attn_out")
        h = _moe(h, moe_norm[i], router_w[i], router_b[i], moe_w1, moe_b1, moe_w2, moe_b2, i)
        h = _ple(h, p[i].reshape(B * S, -1), ple_w[i], ple_norm[i], ple_gate_norm[i],
                 ple_gate_w[i], ple_gate_b[i],
                 final_norm if i == depth - 1 else None)
    return h.reshape(B, S, D)
```

```python
import functools
import math

import jax
import jax.numpy as jnp
from jax import lax
from jax.experimental import pallas as pl
from jax.experimental.pallas import tpu as pltpu

F32 = jnp.float32
BF16 = jnp.bfloat16

RMS_EPS = 1e-6
LN_EPS = 1e-5
CHUNK = 128
GMLP_GROUPS = 16
DIFF_HEAD_DIM = 128
DIFF_V_DIM = 256
ROPE_THETA = 500000.0
ROPE_DIM = DIFF_HEAD_DIM // 4
ROPE_HALF = ROPE_DIM // 2
N_EXPERTS = 32
TOP_K = 4
SWIGLU_ALPHA = 1.702
SWIGLU_LIMIT = 7.0
Q_SCALE = DIFF_HEAD_DIM ** -0.5 * math.log2(math.e)

LANES = 128
VMEM_LIMIT_BYTES = 48 * 1024 * 1024

EXPERT_TILE_M = 512
EXPERT_TILE_COLS = 1024
EXPERT_VMEM_LIMIT_BYTES = 56 * 1024 * 1024


def _params(n_axes, vmem=VMEM_LIMIT_BYTES):
    return pltpu.CompilerParams(
        dimension_semantics=("arbitrary",) * n_axes, vmem_limit_bytes=vmem)


def _rms(x, g):
    var = jnp.mean(x * x, axis=-1, keepdims=True)
    return x * lax.rsqrt(var + RMS_EPS) * g


def _sigmoid(x):
    return 1.0 / (1.0 + jnp.exp(-x))


def _pack_bf16_pairs(x):
    n = x.shape[1] // 2
    bits = lax.bitcast_convert_type(x.astype(BF16).astype(F32), jnp.uint32)
    return (bits[:, :n] >> 16) | bits[:, n:]


def _unpack_bf16_pairs(packed, dtype=BF16):
    lo = lax.bitcast_convert_type(packed << 16, F32)
    hi = lax.bitcast_convert_type(packed & jnp.uint32(0xFFFF0000), F32)
    return lo.astype(dtype), hi.astype(dtype)


def _norm_matmul_body(*refs, norm, has_bias, act, has_res):
    it = iter(refs)
    x_ref = next(it)
    g_ref = next(it) if norm else None
    w_ref = next(it)
    b_ref = next(it) if has_bias else None
    r_ref = next(it) if has_res else None
    o_ref = next(it)
    xn_ref = next(it)

    @pl.when(pl.program_id(1) == 0)
    def _():
        x = x_ref[...].astype(F32)
        if norm:
            x = _rms(x, g_ref[...])
        xn_ref[...] = x.astype(BF16)

    acc = jnp.dot(xn_ref[...], w_ref[...], preferred_element_type=F32)
    if has_bias:
        acc = acc + b_ref[...]
    if act == "gelu":
        acc = 0.5 * acc * (1.0 + lax.erf(acc * (1.0 / math.sqrt(2.0))))
    if has_res:
        acc = acc + r_ref[...]
    o_ref[...] = acc.astype(o_ref.dtype)


def _norm_matmul(x, g, w, *, bias=None, act=None, residual=None, out_dtype=F32,
                 tm=1024, tn=512, name="norm_matmul"):
    M, K = x.shape
    N = w.shape[1]
    norm = g is not None
    in_specs = [pl.BlockSpec((tm, K), lambda i, j: (i, 0))]
    args = [x]
    if norm:
        in_specs.append(pl.BlockSpec((1, K), lambda i, j: (0, 0)))
        args.append(g.reshape(1, K))
    in_specs.append(pl.BlockSpec((K, tn), lambda i, j: (0, j)))
    args.append(w)
    if bias is not None:
        in_specs.append(pl.BlockSpec((1, tn), lambda i, j: (0, j)))
        args.append(bias.reshape(1, N))
    if residual is not None:
        in_specs.append(pl.BlockSpec((tm, tn), lambda i, j: (i, j)))
        args.append(residual)
    body = functools.partial(_norm_matmul_body, norm=norm, has_bias=bias is not None,
                             act=act, has_res=residual is not None)
    return pl.pallas_call(
        body,
        out_shape=jax.ShapeDtypeStruct((M, N), out_dtype),
        grid=(M // tm, N // tn),
        in_specs=in_specs,
        out_specs=pl.BlockSpec((tm, tn), lambda i, j: (i, j)),
        scratch_shapes=[pltpu.VMEM((tm, K), BF16)],
        compiler_params=_params(2),
        name=name,
    )(*args)


def _sgu_body(u_ref, v_ref, lng_ref, lnb_ref, ws_ref, bias_ref, wo_ref, h_ref, o_ref,
              wt_ref, y_ref, *, tm):
    @pl.when(pl.program_id(0) == 0)
    def _():
        row = lax.broadcasted_iota(jnp.int32, (CHUNK, CHUNK), 0)
        col = lax.broadcasted_iota(jnp.int32, (CHUNK, CHUNK), 1)
        causal = col <= row
        for gi in range(GMLP_GROUPS):
            wt_ref[gi] = jnp.where(causal, ws_ref[gi], 0.0).astype(BF16)

    v = v_ref[...].astype(F32)
    mu = jnp.mean(v, axis=-1, keepdims=True)
    vc = v - mu
    var = jnp.mean(vc * vc, axis=-1, keepdims=True)
    vn = (vc * lax.rsqrt(var + LN_EPS) * lng_ref[...] + lnb_ref[...]).astype(BF16)
    for c in range(tm // CHUNK):
        rows = slice(c * CHUNK, (c + 1) * CHUNK)
        for gi in range(GMLP_GROUPS):
            cols = slice(gi * LANES, (gi + 1) * LANES)
            sv = jnp.dot(wt_ref[gi], vn[rows, cols], preferred_element_type=F32)
            sv = sv + bias_ref[:, cols]
            y_ref[rows, cols] = (u_ref[rows, cols].astype(F32) * sv).astype(BF16)
    o_ref[...] = h_ref[...] + jnp.dot(y_ref[...], wo_ref[...], preferred_element_type=F32)


def _sgu(z, ln_g, ln_b, w_s, b_s, w_out, h, *, tm=256):
    T, D = h.shape
    W = z.shape[1] // 2
    bias_tile = jnp.repeat(b_s.T, W // GMLP_GROUPS, axis=1)
    return pl.pallas_call(
        functools.partial(_sgu_body, tm=tm),
        out_shape=jax.ShapeDtypeStruct((T, D), F32),
        grid=(T // tm,),
        in_specs=[
            pl.BlockSpec((tm, W), lambda i: (i, 0)),
            pl.BlockSpec((tm, W), lambda i: (i, 1)),
            pl.BlockSpec((1, W), lambda i: (0, 0)),
            pl.BlockSpec((1, W), lambda i: (0, 0)),
            pl.BlockSpec((GMLP_GROUPS, CHUNK, CHUNK), lambda i: (0, 0, 0)),
            pl.BlockSpec((CHUNK, W), lambda i: (0, 0)),
            pl.BlockSpec((W, D), lambda i: (0, 0)),
            pl.BlockSpec((tm, D), lambda i: (i, 0)),
        ],
        out_specs=pl.BlockSpec((tm, D), lambda i: (i, 0)),
        scratch_shapes=[pltpu.VMEM((GMLP_GROUPS, CHUNK, CHUNK), BF16),
                        pltpu.VMEM((tm, W), BF16)],
        compiler_params=_params(1),
        name="sgu",
    )(z, z, ln_g.reshape(1, W), ln_b.reshape(1, W), w_s, bias_tile, w_out, h)


def _qkv_body(x_ref, g_ref, w_ref, c_ref, s1_ref, s2_ref, o_ref, xn_ref, *, heads_per_tile):
    @pl.when(pl.program_id(1) == 0)
    def _():
        xn_ref[...] = _rms(x_ref[...], g_ref[...]).astype(BF16)

    cosf, s1, s2 = c_ref[...], s1_ref[...], s2_ref[...]
    xn = xn_ref[...]
    for grp in range(heads_per_tile // 2):
        acc = jnp.dot(xn, w_ref[:, grp * 2 * LANES:(grp + 1) * 2 * LANES],
                      preferred_element_type=F32)
        for hh in range(2):
            seg = acc[:, hh * LANES:(hh + 1) * LANES]
            up = pltpu.roll(seg, LANES - ROPE_HALF, 1)
            dn = pltpu.roll(seg, ROPE_HALF, 1)
            cols = slice((2 * grp + hh) * LANES, (2 * grp + hh + 1) * LANES)
            o_ref[:, cols] = (seg * cosf + up * s1 + dn * s2).astype(o_ref.dtype)


def _rope_tables(T):
    inv_freq = jnp.power(ROPE_THETA, -jnp.arange(0, ROPE_DIM, 2, dtype=F32) / ROPE_DIM)
    ang = jnp.arange(T, dtype=F32)[:, None] * inv_freq[None, :]
    cos, sin = jnp.cos(ang), jnp.sin(ang)
    zeros = jnp.zeros((T, LANES - ROPE_DIM), F32)
    half0 = jnp.zeros((T, ROPE_HALF), F32)
    c_tab = jnp.concatenate([cos, cos, jnp.ones((T, LANES - ROPE_DIM), F32)], axis=1)
    s1_tab = jnp.concatenate([-sin, half0, zeros], axis=1)
    s2_tab = jnp.concatenate([half0, sin, zeros], axis=1)
    none = jnp.zeros((T, LANES), F32)
    return (jnp.stack([c_tab * Q_SCALE, c_tab, jnp.ones((T, LANES), F32)]),
            jnp.stack([s1_tab * Q_SCALE, s1_tab, none]),
            jnp.stack([s2_tab * Q_SCALE, s2_tab, none]))


def _qkv_rope(h, g, w, *, tm=1024, tn=512):
    T, D = h.shape
    N = w.shape[1]
    c_tab, s1_tab, s2_tab = _rope_tables(T)
    n_q, n_qk = D // tn, 2 * D // tn

    def kind(j):
        return jnp.where(j < n_q, 0, jnp.where(j < n_qk, 1, 2))

    tab_spec = pl.BlockSpec((None, tm, LANES), lambda i, j: (kind(j), i, 0))
    body = functools.partial(_qkv_body, heads_per_tile=tn // LANES)
    return pl.pallas_call(
        body,
        out_shape=jax.ShapeDtypeStruct((T, N), BF16),
        grid=(T // tm, N // tn),
        in_specs=[pl.BlockSpec((tm, D), lambda i, j: (i, 0)),
                  pl.BlockSpec((1, D), lambda i, j: (0, 0)),
                  pl.BlockSpec((D, tn), lambda i, j: (0, j)),
                  tab_spec, tab_spec, tab_spec],
        out_specs=pl.BlockSpec((tm, tn), lambda i, j: (i, j)),
        scratch_shapes=[pltpu.VMEM((tm, D), BF16)],
        compiler_params=_params(2),
        name="qkv_rope",
    )(h, g.reshape(1, D), w, c_tab, s1_tab, s2_tab)


def _diff_attn_body(lq1_ref, lk1_ref, lq2_ref, lk2_ref, g_ref, q_ref, k_ref, v_ref, o_ref,
                    m0_ref, l0_ref, acc0_ref, m1_ref, l1_ref, acc1_ref, *, tq, lambda_init):
    qi = pl.program_id(1)
    Dh = DIFF_HEAD_DIM
    state = ((m0_ref, l0_ref, acc0_ref), (m1_ref, l1_ref, acc1_ref))
    for m_ref, l_ref, acc_ref in state:
        m_ref[...] = jnp.full(m_ref.shape, -jnp.inf, F32)
        l_ref[...] = jnp.zeros(l_ref.shape, F32)
        acc_ref[...] = jnp.zeros(acc_ref.shape, F32)

    n_rep = tq // LANES

    def chunk(j, masked):
        start = pl.multiple_of(j * tq, tq)
        k = k_ref[pl.ds(start, tq), :]
        v = v_ref[pl.ds(start, tq), :]
        scores = []
        for c in range(2):
            s = lax.dot_general(q_ref[:, c * Dh:(c + 1) * Dh], k[:, c * Dh:(c + 1) * Dh],
                                (((1,), (1,)), ((), ())), preferred_element_type=F32)
            if masked:
                row = lax.broadcasted_iota(jnp.int32, (tq, tq), 0)
                col = lax.broadcasted_iota(jnp.int32, (tq, tq), 1)
                s = jnp.where(col <= row, s, -jnp.inf)
            scores.append(s)
        for s, (m_ref, l_ref, acc_ref) in zip(scores, state):
            m_prev = m_ref[...]
            m_new = jnp.maximum(m_prev, jnp.max(s, axis=-1, keepdims=True))
            p = jnp.exp2(s - jnp.concatenate([m_new] * n_rep, axis=1))
            alpha = jnp.exp2(m_prev - m_new)
            psum = p[:, :LANES]
            for r in range(1, n_rep):
                psum = psum + p[:, r * LANES:(r + 1) * LANES]
            l_ref[...] = alpha * l_ref[...] + psum
            acc_ref[...] = (jnp.concatenate([alpha] * (DIFF_V_DIM // LANES), axis=1) * acc_ref[...]
                            + jnp.dot(p.astype(BF16), v, preferred_element_type=F32))
            m_ref[...] = m_new

    def full_chunk(j, carry):
        chunk(j, False)
        return carry

    lax.fori_loop(0, qi, full_chunk, 0)
    chunk(qi, True)

    lam = (jnp.exp(jnp.sum(lq1_ref[...] * lk1_ref[...], axis=-1, keepdims=True))
           - jnp.exp(jnp.sum(lq2_ref[...] * lk2_ref[...], axis=-1, keepdims=True))
           + lambda_init)
    l0 = jnp.sum(l0_ref[...], axis=-1, keepdims=True)
    l1 = jnp.sum(l1_ref[...], axis=-1, keepdims=True)
    o = acc0_ref[...] / l0 - lam * (acc1_ref[...] / l1)
    o_ref[...] = (_rms(o, g_ref[...]) * (1.0 - lambda_init)).astype(o_ref.dtype)


def _diff_attn(qkv, lq1, lk1, lq2, lk2, subln_g, layer_idx, *, tq=1024):
    T = qkv.shape[0]
    Dv = DIFF_V_DIM
    D = qkv.shape[1] // 3
    H = D // Dv
    lambda_init = 0.8 - 0.6 * math.exp(-0.3 * layer_idx)
    vec = pl.BlockSpec((1, DIFF_HEAD_DIM), lambda h, i: (0, 0))
    body = functools.partial(_diff_attn_body, tq=tq, lambda_init=lambda_init)
    return pl.pallas_call(
        body,
        out_shape=jax.ShapeDtypeStruct((T, D), BF16),
        grid=(H, T // tq),
        in_specs=[vec, vec, vec, vec,
                  pl.BlockSpec((1, Dv), lambda h, i: (0, 0)),
                  pl.BlockSpec((tq, Dv), lambda h, i: (i, h)),
                  pl.BlockSpec((T, Dv), lambda h, i: (0, H + h),
                               pipeline_mode=pl.Buffered(1)),
                  pl.BlockSpec((T, Dv), lambda h, i: (0, 2 * H + h),
                               pipeline_mode=pl.Buffered(1))],
        out_specs=pl.BlockSpec((tq, Dv), lambda h, i: (i, h)),
        scratch_shapes=[pltpu.VMEM((tq, LANES), F32), pltpu.VMEM((tq, LANES), F32),
                        pltpu.VMEM((tq, Dv), F32)] * 2,
        compiler_params=_params(2),
        name="diff_attn",
    )(lq1.reshape(1, -1), lk1.reshape(1, -1), lq2.reshape(1, -1), lk2.reshape(1, -1),
      subln_g.reshape(1, Dv), qkv, qkv, qkv)


def _router_body(h_ref, g_ref, rw_ref, rb_ref, xs_ref, mi_ref, mf_ref, cnt_ref, run_ref, *, tm):
    @pl.when(pl.program_id(0) == 0)
    def _():
        run_ref[...] = jnp.zeros(run_ref.shape, F32)

    xs = _rms(h_ref[...], g_ref[...])
    xs_ref[...] = _pack_bf16_pairs(xs)
    xs_hi = xs.astype(BF16)
    xs_lo = (xs - xs_hi.astype(F32)).astype(BF16)
    rw_hi = rw_ref[...].astype(BF16)
    rw_lo = (rw_ref[...] - rw_hi.astype(F32)).astype(BF16)
    logits = (jnp.dot(xs_hi, rw_hi, preferred_element_type=F32)
              + jnp.dot(xs_hi, rw_lo, preferred_element_type=F32)
              + jnp.dot(xs_lo, rw_hi, preferred_element_type=F32)) + rb_ref[...]
    lane = lax.broadcasted_iota(jnp.int32, (tm, LANES), 1)
    work = jnp.where(lane < N_EXPERTS, logits, -jnp.inf)
    vals, idxs, hots = [], [], []
    for _ in range(TOP_K):
        mx = jnp.max(work, axis=-1, keepdims=True)
        idx = jnp.min(jnp.where(work == mx, lane, LANES), axis=-1, keepdims=True)
        hot = lane == idx
        vals.append(mx)
        idxs.append(idx)
        hots.append(hot)
        work = jnp.where(hot, -jnp.inf, work)
    exps = [jnp.exp(v - vals[0]) for v in vals]
    denom = exps[0] + exps[1] + exps[2] + exps[3]

    sel = jnp.zeros((tm, LANES), F32)
    for hot in hots:
        sel = sel + hot.astype(F32)
    row = lax.broadcasted_iota(jnp.int32, (tm, tm), 0)
    col = lax.broadcasted_iota(jnp.int32, (tm, tm), 1)
    before = (col < row).astype(BF16)
    rank_all = jnp.dot(before, sel.astype(BF16), preferred_element_type=F32) + run_ref[...]
    run_ref[...] = run_ref[...] + jnp.sum(sel, axis=0, keepdims=True)
    cnt_ref[...] = run_ref[...].astype(jnp.int32)

    mi = jnp.zeros((tm, LANES), jnp.int32)
    mf = jnp.zeros((tm, LANES), F32)
    for k in range(TOP_K):
        rank_k = jnp.sum(jnp.where(hots[k], rank_all, 0.0), axis=-1, keepdims=True)
        mi = jnp.where(lane == k, idxs[k], mi)
        mi = jnp.where(lane == TOP_K + k, rank_k.astype(jnp.int32), mi)
        mf = jnp.where(lane == k, exps[k] / denom, mf)
    mi_ref[...] = mi
    mf_ref[...] = mf


def _router(h, g, rw, rb, *, tm=512):
    T, D = h.shape
    rw_pad = jnp.zeros((D, LANES), F32).at[:, :N_EXPERTS].set(rw)
    rb_pad = jnp.zeros((1, LANES), F32).at[0, :N_EXPERTS].set(rb)
    return pl.pallas_call(
        functools.partial(_router_body, tm=tm),
        out_shape=(jax.ShapeDtypeStruct((T, D // 2), jnp.uint32),
                   jax.ShapeDtypeStruct((T, LANES), jnp.int32),
                   jax.ShapeDtypeStruct((T, LANES), F32),
                   jax.ShapeDtypeStruct((1, LANES), jnp.int32)),
        grid=(T // tm,),
        in_specs=[pl.BlockSpec((tm, D), lambda i: (i, 0)),
                  pl.BlockSpec((1, D), lambda i: (0, 0)),
                  pl.BlockSpec((D, LANES), lambda i: (0, 0)),
                  pl.BlockSpec((1, LANES), lambda i: (0, 0))],
        out_specs=(pl.BlockSpec((tm, D // 2), lambda i: (i, 0)),
                   pl.BlockSpec((tm, LANES), lambda i: (i, 0)),
                   pl.BlockSpec((tm, LANES), lambda i: (i, 0)),
                   pl.BlockSpec((1, LANES), lambda i: (0, 0))),
        scratch_shapes=[pltpu.VMEM((1, LANES), F32)],
        compiler_params=_params(1),
        name="router",
    )(h, g.reshape(1, D), rw_pad, rb_pad)


def _dispatch_body(slot_ref, xs_ref, xb_in_ref, xb_ref, sem, *, tm):
    del xb_in_ref
    base = pl.program_id(0) * (tm * TOP_K)

    def row_copy(r, s):
        return pltpu.make_async_copy(xs_ref.at[pl.ds(r, 1), :], xb_ref.at[pl.ds(s, 1), :], sem)

    def issue(r, carry):
        for k in range(TOP_K):
            row_copy(r, slot_ref[base + r * TOP_K + k]).start(priority=k % 2)
        return carry

    lax.fori_loop(0, tm, issue, 0, unroll=8)
    for _ in range(TOP_K):
        pltpu.make_async_copy(xs_ref, xb_ref.at[pl.ds(0, tm), :], sem).wait()


def _dispatch(slot_flat, xs, n_rows, *, tm=512):
    T, D = xs.shape
    xb_init = jnp.zeros((n_rows, D), xs.dtype)
    return pl.pallas_call(
        functools.partial(_dispatch_body, tm=tm),
        out_shape=jax.ShapeDtypeStruct((n_rows, D), xs.dtype),
        grid_spec=pltpu.PrefetchScalarGridSpec(
            num_scalar_prefetch=1,
            grid=(T // tm,),
            in_specs=[pl.BlockSpec((tm, D), lambda i, s: (i, 0)),
                      pl.BlockSpec(memory_space=pl.ANY)],
            out_specs=pl.BlockSpec(memory_space=pl.ANY),
            scratch_shapes=[pltpu.SemaphoreType.DMA],
        ),
        input_output_aliases={2: 0},
        compiler_params=_params(1),
        name="moe_dispatch",
    )(slot_flat, xs, xb_init)


def _expert_steps(tiles, n_col, max_tiles):
    E = tiles.shape[0]
    tend = jnp.cumsum(tiles)
    tstart = tend - tiles
    n_used_steps = n_col * tend[-1]
    s = jnp.arange(n_col * max_tiles, dtype=jnp.int32)
    s_eff = jnp.minimum(s, jnp.maximum(n_used_steps - 1, 0))
    e_s = jnp.minimum(
        jnp.sum((n_col * tend[None, :] <= s_eff[:, None]).astype(jnp.int32), axis=1), E - 1)
    q = s_eff - n_col * tstart[e_s]
    per = jnp.maximum(tiles[e_s], 1)
    wcol = jnp.clip(q // per, 0, n_col - 1)
    r = q - wcol * per
    used = s < n_used_steps
    first = jnp.logical_and(used, r == 0)
    spare = jnp.maximum(s - n_used_steps, 0)
    col = jnp.where(used, wcol, spare % n_col)
    tile = jnp.where(used, tstart[e_s] + r, tend[-1] + spare // n_col)
    i32 = lambda a: a.astype(jnp.int32)
    return i32(e_s), i32(wcol), i32(col), i32(tile), i32(used), i32(first)


def _expert_up_body(ex_ref, wcol_ref, col_ref, tile_ref, used_ref, first_ref, x_ref, w1g_ref,
                    w1l_ref, b1g_ref, b1l_ref, h_ref, wg_ref, wl_ref):
    del ex_ref, wcol_ref, col_ref, tile_ref
    s = pl.program_id(0)

    @pl.when(used_ref[s] == 0)
    def _():
        h_ref[...] = jnp.zeros(h_ref.shape, h_ref.dtype)

    @pl.when(first_ref[s] == 1)
    def _():
        wg_ref[...] = w1g_ref[...].astype(BF16)
        wl_ref[...] = w1l_ref[...].astype(BF16)

    @pl.when(used_ref[s] == 1)
    def _():
        lo, hi = _unpack_bf16_pairs(x_ref[...])
        x = jnp.concatenate([lo, hi], axis=1)
        glu = jnp.dot(x, wg_ref[...], preferred_element_type=F32) + b1g_ref[...]
        lin = jnp.dot(x, wl_ref[...], preferred_element_type=F32) + b1l_ref[...]
        glu = jnp.minimum(glu, SWIGLU_LIMIT)
        lin = jnp.clip(lin, -SWIGLU_LIMIT, SWIGLU_LIMIT)
        h_ref[...] = (glu * _sigmoid(SWIGLU_ALPHA * glu) * (lin + 1.0)).astype(h_ref.dtype)


def _expert_down_body(ex_ref, wcol_ref, col_ref, tile_ref, used_ref, first_ref, h_ref, w2_ref,
                      b2_ref, o_ref, wd_ref):
    del ex_ref, wcol_ref, col_ref, tile_ref
    s = pl.program_id(0)

    @pl.when(used_ref[s] == 0)
    def _():
        o_ref[...] = jnp.zeros(o_ref.shape, o_ref.dtype)

    @pl.when(first_ref[s] == 1)
    def _():
        wd_ref[...] = w2_ref[...].astype(BF16)

    @pl.when(used_ref[s] == 1)
    def _():
        y = jnp.dot(h_ref[...], wd_ref[...], preferred_element_type=F32) + b2_ref[...]
        o_ref[...] = _pack_bf16_pairs(y)


def _expert_ffn(tiles, xb, w1, b1, w2, b2, layer):
    P, Dh = xb.shape
    D = 2 * Dh
    _, E, _, F2 = w1.shape
    F = F2 // 2
    tm, tc = EXPERT_TILE_M, EXPERT_TILE_COLS
    max_tiles = P // tm
    b1r = b1.reshape(-1, E, 1, F2)
    b2r = b2.reshape(-1, E, 1, D)

    nf = F // tc
    steps = _expert_steps(tiles, nf, max_tiles)
    hidden = pl.pallas_call(
        _expert_up_body,
        out_shape=jax.ShapeDtypeStruct((P, F), BF16),
        grid_spec=pltpu.PrefetchScalarGridSpec(
            num_scalar_prefetch=6,
            grid=(nf * max_tiles,),
            in_specs=[
                pl.BlockSpec((tm, Dh), lambda s, ex, wc, col, tl, us, fr: (tl[s], 0)),
                pl.BlockSpec((None, None, D, tc),
                             lambda s, ex, wc, col, tl, us, fr: (layer, ex[s], 0, wc[s])),
                pl.BlockSpec((None, None, D, tc),
                             lambda s, ex, wc, col, tl, us, fr: (layer, ex[s], 0, nf + wc[s])),
                pl.BlockSpec((None, None, 1, tc),
                             lambda s, ex, wc, col, tl, us, fr: (layer, ex[s], 0, wc[s])),
                pl.BlockSpec((None, None, 1, tc),
                             lambda s, ex, wc, col, tl, us, fr: (layer, ex[s], 0, nf + wc[s])),
            ],
            out_specs=pl.BlockSpec((tm, tc), lambda s, ex, wc, col, tl, us, fr: (tl[s], col[s])),
            scratch_shapes=[pltpu.VMEM((D, tc), BF16), pltpu.VMEM((D, tc), BF16)],
        ),
        compiler_params=_params(1, EXPERT_VMEM_LIMIT_BYTES),
        name="moe_up",
    )(*steps, xb, w1, w1, b1r, b1r)

    steps = _expert_steps(tiles, 1, max_tiles)
    return pl.pallas_call(
        _expert_down_body,
        out_shape=jax.ShapeDtypeStruct((P, Dh), jnp.uint32),
        grid_spec=pltpu.PrefetchScalarGridSpec(
            num_scalar_prefetch=6,
            grid=(max_tiles,),
            in_specs=[
                pl.BlockSpec((tm, F), lambda s, ex, wc, col, tl, us, fr: (tl[s], 0)),
                pl.BlockSpec((None, None, F, D),
                             lambda s, ex, wc, col, tl, us, fr: (layer, ex[s], 0, 0)),
                pl.BlockSpec((None, None, 1, D),
                             lambda s, ex, wc, col, tl, us, fr: (layer, ex[s], 0, 0)),
            ],
            out_specs=pl.BlockSpec((tm, Dh), lambda s, ex, wc, col, tl, us, fr: (tl[s], 0)),
            scratch_shapes=[pltpu.VMEM((F, D), BF16)],
        ),
        compiler_params=_params(1, EXPERT_VMEM_LIMIT_BYTES),
        name="moe_down",
    )(*steps, hidden, w2, b2r)


def _combine_body(slot_ref, yb_ref, gate_ref, h_ref, o_ref, buf_ref, sems, *, tm, n_steps):
    i = pl.program_id(0)

    def start_gather(step, buf):
        base = step * (tm * TOP_K)

        def issue(r, carry):
            for k in range(TOP_K):
                s = slot_ref[base + r * TOP_K + k]
                pltpu.make_async_copy(yb_ref.at[pl.ds(s, 1), :],
                                      buf_ref.at[buf, k, pl.ds(r, 1), :],
                                      sems.at[buf]).start(priority=k % 2)
            return carry

        lax.fori_loop(0, tm, issue, 0, unroll=8)

    @pl.when(i == 0)
    def _():
        start_gather(0, 0)

    for buf in range(2):
        @pl.when(jnp.logical_and(i + 1 < n_steps, (i + 1) % 2 == buf))
        def _():
            start_gather(i + 1, buf)

    cur = i % 2
    for k in range(TOP_K):
        pltpu.make_async_copy(yb_ref.at[pl.ds(0, tm), :], buf_ref.at[cur, k], sems.at[cur]).wait()
    half = buf_ref.shape[-1]
    acc_lo = h_ref[:, :half]
    acc_hi = h_ref[:, half:]
    gates = gate_ref[...]
    for k in range(TOP_K):
        lo, hi = _unpack_bf16_pairs(buf_ref[cur, k], F32)
        acc_lo = acc_lo + gates[:, k:k + 1] * lo
        acc_hi = acc_hi + gates[:, k:k + 1] * hi
    o_ref[:, :half] = acc_lo
    o_ref[:, half:] = acc_hi


def _combine(slot_flat, yb, gates, h, *, tm=256):
    T, D = h.shape
    n_steps = T // tm
    return pl.pallas_call(
        functools.partial(_combine_body, tm=tm, n_steps=n_steps),
        out_shape=jax.ShapeDtypeStruct((T, D), F32),
        grid_spec=pltpu.PrefetchScalarGridSpec(
            num_scalar_prefetch=1,
            grid=(n_steps,),
            in_specs=[pl.BlockSpec(memory_space=pl.ANY),
                      pl.BlockSpec((tm, LANES), lambda i, s: (i, 0)),
                      pl.BlockSpec((tm, D), lambda i, s: (i, 0))],
            out_specs=pl.BlockSpec((tm, D), lambda i, s: (i, 0)),
            scratch_shapes=[pltpu.VMEM((2, TOP_K, tm, D // 2), jnp.uint32),
                            pltpu.SemaphoreType.DMA((2,))],
        ),
        compiler_params=_params(1),
        name="moe_combine",
    )(slot_flat, yb, gates, h)


def _moe(h, norm_g, router_w, router_b, w1, b1, w2, b2, layer):
    T, D = h.shape
    E, tm = N_EXPERTS, EXPERT_TILE_M
    xs, meta_i, meta_f, cnt = _router(h, norm_g, router_w, router_b)
    expert = meta_i[:, :TOP_K]
    rank = meta_i[:, TOP_K:2 * TOP_K]
    counts = cnt[0, :E]
    padded = (counts + tm - 1) // tm * tm
    pend = jnp.cumsum(padded)
    pstart = pend - padded
    slot_flat = (pstart[expert] + rank).reshape(-1).astype(jnp.int32)
    max_tiles = (T * TOP_K) // tm + E
    xb = _dispatch(slot_flat, xs, max_tiles * tm)
    yb = _expert_ffn(padded // tm, xb, w1, b1, w2, b2, layer)
    return _combine(slot_flat, yb, meta_f, h)


def _ple_body(*refs, final):
    (h_ref, p_ref, wp_ref, pn_ref, gn_ref, gw_ref, gb_ref) = refs[:7]
    fn_ref = refs[7] if final else None
    o_ref = refs[-1]
    h = h_ref[...]
    e = jnp.dot(p_ref[...].astype(BF16), wp_ref[...], preferred_element_type=F32)
    e = _rms(e, pn_ref[...])
    hn = _rms(h, gn_ref[...]).astype(BF16)
    gate = _sigmoid(jnp.dot(hn, gw_ref[...], preferred_element_type=F32) + gb_ref[...])
    out = h + gate * e
    if final:
        out = _rms(out, fn_ref[...])
    o_ref[...] = out


def _ple(h, p_i, w_p, p_norm, gate_norm, gate_w, gate_b, final_norm=None, *, tm=512):
    T, D = h.shape
    Pd = p_i.shape[1]
    final = final_norm is not None
    vec = pl.BlockSpec((1, D), lambda i: (0, 0))
    in_specs = [pl.BlockSpec((tm, D), lambda i: (i, 0)),
                pl.BlockSpec((tm, Pd), lambda i: (i, 0)),
                pl.BlockSpec((Pd, D), lambda i: (0, 0)),
                vec, vec,
                pl.BlockSpec((D, D), lambda i: (0, 0)),
                vec]
    args = [h, p_i, w_p.astype(BF16), p_norm.reshape(1, D), gate_norm.reshape(1, D),
            gate_w.astype(BF16), gate_b.reshape(1, D)]
    if final:
        in_specs.append(vec)
        args.append(final_norm.reshape(1, D))
    return pl.pallas_call(
        functools.partial(_ple_body, final=final),
        out_shape=jax.ShapeDtypeStruct((T, D), F32),
        grid=(T // tm,),
        in_specs=in_specs,
        out_specs=pl.BlockSpec((tm, D), lambda i: (i, 0)),
        compiler_params=_params(1),
        name="ple",
    )(*args)


def kernel(x, p, a_norm, a_w_in, a_ln_g, a_ln_b, a_w_s, a_b_s, a_w_out, b_norm, b_w_qkv, b_lq1, b_lk1, b_lq2, b_lk2, b_subln, b_w_out, moe_norm, router_w, router_b, moe_w1, moe_b1, moe_w2, moe_b2, ple_w, ple_norm, ple_gate_norm, ple_gate_w, ple_gate_b, final_norm):
    B, S, D = x.shape
    assert B == 1, "attention and chunked mixing treat the row axis as one sequence"
    depth = p.shape[0]
    h = x.reshape(B * S, D)
    for i in range(depth):
        j = i // 2
        if i % 2 == 0:
            z = _norm_matmul(h, a_norm[j], a_w_in[j].astype(BF16), act="gelu",
                             out_dtype=BF16, name="gmlp_in")
            h = _sgu(z, a_ln_g[j], a_ln_b[j], a_w_s[j], a_b_s[j], a_w_out[j].astype(BF16), h)
        else:
            qkv = _qkv_rope(h, b_norm[j], b_w_qkv[j].astype(BF16))
            o = _diff_attn(qkv, b_lq1[j], b_lk1[j], b_lq2[j], b_lk2[j], b_subln[j], i)
            h = _norm_matmul(o, None, b_w_out[j].astype(BF16), residual=h, name="attn_out")
        h = _moe(h, moe_norm[i], router_w[i], router_b[i], moe_w1, moe_b1, moe_w2, moe_b2, i)
        h = _ple(h, p[i].reshape(B * S, -1), ple_w[i], ple_norm[i], ple_gate_norm[i],
                 ple_gate_w[i], ple_gate_b[i],
                 final_norm if i == depth - 1 else None)
    return h.reshape(B, S, D)
```

```python
import functools
import math

import jax
import jax.numpy as jnp
from jax import lax
from jax.experimental import pallas as pl
from jax.experimental.pallas import tpu as pltpu

F32 = jnp.float32
BF16 = jnp.bfloat16

RMS_EPS = 1e-6
LN_EPS = 1e-5
CHUNK = 128
GMLP_GROUPS = 16
DIFF_HEAD_DIM = 128
DIFF_V_DIM = 256
ROPE_THETA = 500000.0
ROPE_DIM = DIFF_HEAD_DIM // 4
ROPE_HALF = ROPE_DIM // 2
N_EXPERTS = 32
TOP_K = 4
SWIGLU_ALPHA = 1.702
SWIGLU_LIMIT = 7.0
Q_SCALE = DIFF_HEAD_DIM ** -0.5 * math.log2(math.e)

LANES = 128
VMEM_LIMIT_BYTES = 48 * 1024 * 1024

EXPERT_TILE_M = 512
EXPERT_TILE_COLS = 1024
EXPERT_VMEM_LIMIT_BYTES = 56 * 1024 * 1024


def _params(n_axes, vmem=VMEM_LIMIT_BYTES):
    return pltpu.CompilerParams(
        dimension_semantics=("arbitrary",) * n_axes, vmem_limit_bytes=vmem)


def _rms(x, g):
    var = jnp.mean(x * x, axis=-1, keepdims=True)
    return x * lax.rsqrt(var + RMS_EPS) * g


def _sigmoid(x):
    return 1.0 / (1.0 + jnp.exp(-x))


def _pack_bf16_pairs(x):
    n = x.shape[1] // 2
    bits = lax.bitcast_convert_type(x.astype(BF16).astype(F32), jnp.uint32)
    return (bits[:, :n] >> 16) | bits[:, n:]


def _unpack_bf16_pairs(packed, dtype=BF16):
    lo = lax.bitcast_convert_type(packed << 16, F32)
    hi = lax.bitcast_convert_type(packed & jnp.uint32(0xFFFF0000), F32)
    return lo.astype(dtype), hi.astype(dtype)


def _norm_matmul_body(*refs, norm, has_bias, act, has_res):
    it = iter(refs)
    x_ref = next(it)
    g_ref = next(it) if norm else None
    w_ref = next(it)
    b_ref = next(it) if has_bias else None
    r_ref = next(it) if has_res else None
    o_ref = next(it)
    xn_ref = next(it)

    @pl.when(pl.program_id(1) == 0)
    def _():
        x = x_ref[...].astype(F32)
        if norm:
            x = _rms(x, g_ref[...])
        xn_ref[...] = x.astype(BF16)

    acc = jnp.dot(xn_ref[...], w_ref[...], preferred_element_type=F32)
    if has_bias:
        acc = acc + b_ref[...]
    if act == "gelu":
        acc = 0.5 * acc * (1.0 + lax.erf(acc * (1.0 / math.sqrt(2.0))))
    if has_res:
        acc = acc + r_ref[...]
    o_ref[...] = acc.astype(o_ref.dtype)


def _norm_matmul(x, g, w, *, bias=None, act=None, residual=None, out_dtype=F32,
                 tm=1024, tn=512, name="norm_matmul"):
    M, K = x.shape
    N = w.shape[1]
    norm = g is not None
    in_specs = [pl.BlockSpec((tm, K), lambda i, j: (i, 0))]
    args = [x]
    if norm:
        in_specs.append(pl.BlockSpec((1, K), lambda i, j: (0, 0)))
        args.append(g.reshape(1, K))
    in_specs.append(pl.BlockSpec((K, tn), lambda i, j: (0, j)))
    args.append(w)
    if bias is not None:
        in_specs.append(pl.BlockSpec((1, tn), lambda i, j: (0, j)))
        args.append(bias.reshape(1, N))
    if residual is not None:
        in_specs.append(pl.BlockSpec((tm, tn), lambda i, j: (i, j)))
        args.append(residual)
    body = functools.partial(_norm_matmul_body, norm=norm, has_bias=bias is not None,
                             act=act, has_res=residual is not None)
    return pl.pallas_call(
        body,
        out_shape=jax.ShapeDtypeStruct((M, N), out_dtype),
        grid=(M // tm, N // tn),
        in_specs=in_specs,
        out_specs=pl.BlockSpec((tm, tn), lambda i, j: (i, j)),
        scratch_shapes=[pltpu.VMEM((tm, K), BF16)],
        compiler_params=_params(2),
        name=name,
    )(*args)


def _sgu_body(u_ref, v_ref, lng_ref, lnb_ref, ws_ref, bias_ref, wo_ref, h_ref, o_ref,
              wt_ref, y_ref, *, tm):
    @pl.when(pl.program_id(0) == 0)
    def _():
        row = lax.broadcasted_iota(jnp.int32, (CHUNK, CHUNK), 0)
        col = lax.broadcasted_iota(jnp.int32, (CHUNK, CHUNK), 1)
        causal = col <= row
        for gi in range(GMLP_GROUPS):
            wt_ref[gi] = jnp.where(causal, ws_ref[gi], 0.0).astype(BF16)

    v = v_ref[...].astype(F32)
    mu = jnp.mean(v, axis=-1, keepdims=True)
    vc = v - mu
    var = jnp.mean(vc * vc, axis=-1, keepdims=True)
    vn = (vc * lax.rsqrt(var + LN_EPS) * lng_ref[...] + lnb_ref[...]).astype(BF16)
    for c in range(tm // CHUNK):
        rows = slice(c * CHUNK, (c + 1) * CHUNK)
        for gi in range(GMLP_GROUPS):
            cols = slice(gi * LANES, (gi + 1) * LANES)
            sv = jnp.dot(wt_ref[gi], vn[rows, cols], preferred_element_type=F32)
            sv = sv + bias_ref[:, cols]
            y_ref[rows, cols] = (u_ref[rows, cols].astype(F32) * sv).astype(BF16)
    o_ref[...] = h_ref[...] + jnp.dot(y_ref[...], wo_ref[...], preferred_element_type=F32)


def _sgu(z, ln_g, ln_b, w_s, b_s, w_out, h, *, tm=256):
    T, D = h.shape
    W = z.shape[1] // 2
    bias_tile = jnp.repeat(b_s.T, W // GMLP_GROUPS, axis=1)
    return pl.pallas_call(
        functools.partial(_sgu_body, tm=tm),
        out_shape=jax.ShapeDtypeStruct((T, D), F32),
        grid=(T // tm,),
        in_specs=[
            pl.BlockSpec((tm, W), lambda i: (i, 0)),
            pl.BlockSpec((tm, W), lambda i: (i, 1)),
            pl.BlockSpec((1, W), lambda i: (0, 0)),
            pl.BlockSpec((1, W), lambda i: (0, 0)),
            pl.BlockSpec((GMLP_GROUPS, CHUNK, CHUNK), lambda i: (0, 0, 0)),
            pl.BlockSpec((CHUNK, W), lambda i: (0, 0)),
            pl.BlockSpec((W, D), lambda i: (0, 0)),
            pl.BlockSpec((tm, D), lambda i: (i, 0)),
        ],
        out_specs=pl.BlockSpec((tm, D), lambda i: (i, 0)),
        scratch_shapes=[pltpu.VMEM((GMLP_GROUPS, CHUNK, CHUNK), BF16),
                        pltpu.VMEM((tm, W), BF16)],
        compiler_params=_params(1),
        name="sgu",
    )(z, z, ln_g.reshape(1, W), ln_b.reshape(1, W), w_s, bias_tile, w_out, h)


def _qkv_body(x_ref, g_ref, w_ref, c_ref, s1_ref, s2_ref, o_ref, xn_ref, *,
              n_q_tiles, n_qk_tiles, heads_per_tile):
    j = pl.program_id(1)

    @pl.when(j == 0)
    def _():
        xn_ref[...] = _rms(x_ref[...], g_ref[...]).astype(BF16)

    is_v = j >= n_qk_tiles
    scale = jnp.where(j < n_q_tiles, Q_SCALE, 1.0).astype(F32)
    cosf = jnp.where(is_v, 1.0, c_ref[...] * scale)
    s1 = jnp.where(is_v, 0.0, s1_ref[...] * scale)
    s2 = jnp.where(is_v, 0.0, s2_ref[...] * scale)
    xn = xn_ref[...]
    for grp in range(heads_per_tile // 2):
        acc = jnp.dot(xn, w_ref[:, grp * 2 * LANES:(grp + 1) * 2 * LANES],
                      preferred_element_type=F32)
        for hh in range(2):
            seg = acc[:, hh * LANES:(hh + 1) * LANES]
            up = pltpu.roll(seg, LANES - ROPE_HALF, 1)
            dn = pltpu.roll(seg, ROPE_HALF, 1)
            cols = slice((2 * grp + hh) * LANES, (2 * grp + hh + 1) * LANES)
            o_ref[:, cols] = (seg * cosf + up * s1 + dn * s2).astype(o_ref.dtype)


def _rope_tables(T):
    lane = jnp.arange(LANES, dtype=jnp.int32)[None, :]
    inv_freq = jnp.power(ROPE_THETA, -(2.0 * (lane % ROPE_HALF).astype(F32)) / ROPE_DIM)
    ang = jnp.arange(T, dtype=F32)[:, None] * inv_freq
    cos, sin = jnp.cos(ang), jnp.sin(ang)
    c_tab = jnp.where(lane < ROPE_DIM, cos, 1.0)
    s1_tab = jnp.where(lane < ROPE_HALF, -sin, 0.0)
    s2_tab = jnp.where(jnp.logical_and(lane >= ROPE_HALF, lane < ROPE_DIM), sin, 0.0)
    return c_tab, s1_tab, s2_tab


def _qkv_rope(h, g, w, *, tm=1024, tn=512):
    T, D = h.shape
    N = w.shape[1]
    c_tab, s1_tab, s2_tab = _rope_tables(T)
    tab_spec = pl.BlockSpec((tm, LANES), lambda i, j: (i, 0))
    body = functools.partial(_qkv_body, n_q_tiles=D // tn, n_qk_tiles=2 * D // tn,
                             heads_per_tile=tn // LANES)
    return pl.pallas_call(
        body,
        out_shape=jax.ShapeDtypeStruct((T, N), BF16),
        grid=(T // tm, N // tn),
        in_specs=[pl.BlockSpec((tm, D), lambda i, j: (i, 0)),
                  pl.BlockSpec((1, D), lambda i, j: (0, 0)),
                  pl.BlockSpec((D, tn), lambda i, j: (0, j)),
                  tab_spec, tab_spec, tab_spec],
        out_specs=pl.BlockSpec((tm, tn), lambda i, j: (i, j)),
        scratch_shapes=[pltpu.VMEM((tm, D), BF16)],
        compiler_params=_params(2),
        name="qkv_rope",
    )(h, g.reshape(1, D), w, c_tab, s1_tab, s2_tab)


def _diff_attn_body(lq1_ref, lk1_ref, lq2_ref, lk2_ref, g_ref, q_ref, k_ref, v_ref, o_ref,
                    m0_ref, l0_ref, acc0_ref, m1_ref, l1_ref, acc1_ref, *, tq, lambda_init):
    qi = pl.program_id(1)
    Dh = DIFF_HEAD_DIM
    state = ((m0_ref, l0_ref, acc0_ref), (m1_ref, l1_ref, acc1_ref))
    for m_ref, l_ref, acc_ref in state:
        m_ref[...] = jnp.full(m_ref.shape, -jnp.inf, F32)
        l_ref[...] = jnp.zeros(l_ref.shape, F32)
        acc_ref[...] = jnp.zeros(acc_ref.shape, F32)

    n_rep = tq // LANES

    def chunk(j, masked):
        start = pl.multiple_of(j * tq, tq)
        k = k_ref[pl.ds(start, tq), :]
        v = v_ref[pl.ds(start, tq), :]
        scores = []
        for c in range(2):
            s = lax.dot_general(q_ref[:, c * Dh:(c + 1) * Dh], k[:, c * Dh:(c + 1) * Dh],
                                (((1,), (1,)), ((), ())), preferred_element_type=F32)
            if masked:
                row = lax.broadcasted_iota(jnp.int32, (tq, tq), 0)
                col = lax.broadcasted_iota(jnp.int32, (tq, tq), 1)
                s = jnp.where(col <= row, s, -jnp.inf)
            scores.append(s)
        for s, (m_ref, l_ref, acc_ref) in zip(scores, state):
            m_prev = m_ref[...]
            m_new = jnp.maximum(m_prev, jnp.max(s, axis=-1, keepdims=True))
            p = jnp.exp2(s - jnp.concatenate([m_new] * n_rep, axis=1))
            alpha = jnp.exp2(m_prev - m_new)
            psum = p[:, :LANES]
            for r in range(1, n_rep):
                psum = psum + p[:, r * LANES:(r + 1) * LANES]
            l_ref[...] = alpha * l_ref[...] + psum
            acc_ref[...] = (jnp.concatenate([alpha] * (DIFF_V_DIM // LANES), axis=1) * acc_ref[...]
                            + jnp.dot(p.astype(BF16), v, preferred_element_type=F32))
            m_ref[...] = m_new

    def full_chunk(j, carry):
        chunk(j, False)
        return carry

    lax.fori_loop(0, qi, full_chunk, 0)
    chunk(qi, True)

    lam = (jnp.exp(jnp.sum(lq1_ref[...] * lk1_ref[...], axis=-1, keepdims=True))
           - jnp.exp(jnp.sum(lq2_ref[...] * lk2_ref[...], axis=-1, keepdims=True))
           + lambda_init)
    l0 = jnp.sum(l0_ref[...], axis=-1, keepdims=True)
    l1 = jnp.sum(l1_ref[...], axis=-1, keepdims=True)
    o = acc0_ref[...] / l0 - lam * (acc1_ref[...] / l1)
    o_ref[...] = (_rms(o, g_ref[...]) * (1.0 - lambda_init)).astype(o_ref.dtype)


def _diff_attn(qkv, lq1, lk1, lq2, lk2, subln_g, layer_idx, *, tq=1024):
    T = qkv.shape[0]
    Dv = DIFF_V_DIM
    D = qkv.shape[1] // 3
    H = D // Dv
    lambda_init = 0.8 - 0.6 * math.exp(-0.3 * layer_idx)
    vec = pl.BlockSpec((1, DIFF_HEAD_DIM), lambda h, i: (0, 0))
    body = functools.partial(_diff_attn_body, tq=tq, lambda_init=lambda_init)
    return pl.pallas_call(
        body,
        out_shape=jax.ShapeDtypeStruct((T, D), BF16),
        grid=(H, T // tq),
        in_specs=[vec, vec, vec, vec,
                  pl.BlockSpec((1, Dv), lambda h, i: (0, 0)),
                  pl.BlockSpec((tq, Dv), lambda h, i: (i, h)),
                  pl.BlockSpec((T, Dv), lambda h, i: (0, H + h),
                               pipeline_mode=pl.Buffered(1)),
                  pl.BlockSpec((T, Dv), lambda h, i: (0, 2 * H + h),
                               pipeline_mode=pl.Buffered(1))],
        out_specs=pl.BlockSpec((tq, Dv), lambda h, i: (i, h)),
        scratch_shapes=[pltpu.VMEM((tq, LANES), F32), pltpu.VMEM((tq, LANES), F32),
                        pltpu.VMEM((tq, Dv), F32)] * 2,
        compiler_params=_params(2),
        name="diff_attn",
    )(lq1.reshape(1, -1), lk1.reshape(1, -1), lq2.reshape(1, -1), lk2.reshape(1, -1),
      subln_g.reshape(1, Dv), qkv, qkv, qkv)


def _router_body(h_ref, g_ref, rw_ref, rb_ref, xs_ref, mi_ref, mf_ref, cnt_ref, run_ref, *, tm):
    @pl.when(pl.program_id(0) == 0)
    def _():
        run_ref[...] = jnp.zeros(run_ref.shape, F32)

    xs = _rms(h_ref[...], g_ref[...])
    xs_ref[...] = _pack_bf16_pairs(xs)
    xs_hi = xs.astype(BF16)
    xs_lo = (xs - xs_hi.astype(F32)).astype(BF16)
    rw_hi = rw_ref[...].astype(BF16)
    rw_lo = (rw_ref[...] - rw_hi.astype(F32)).astype(BF16)
    logits = (jnp.dot(xs_hi, rw_hi, preferred_element_type=F32)
              + jnp.dot(xs_hi, rw_lo, preferred_element_type=F32)
              + jnp.dot(xs_lo, rw_hi, preferred_element_type=F32)) + rb_ref[...]
    lane = lax.broadcasted_iota(jnp.int32, (tm, LANES), 1)
    work = jnp.where(lane < N_EXPERTS, logits, -jnp.inf)
    vals, idxs, hots = [], [], []
    for _ in range(TOP_K):
        mx = jnp.max(work, axis=-1, keepdims=True)
        idx = jnp.min(jnp.where(work == mx, lane, LANES), axis=-1, keepdims=True)
        hot = lane == idx
        vals.append(mx)
        idxs.append(idx)
        hots.append(hot)
        work = jnp.where(hot, -jnp.inf, work)
    exps = [jnp.exp(v - vals[0]) for v in vals]
    denom = exps[0] + exps[1] + exps[2] + exps[3]

    sel = jnp.zeros((tm, LANES), F32)
    for hot in hots:
        sel = sel + hot.astype(F32)
    row = lax.broadcasted_iota(jnp.int32, (tm, tm), 0)
    col = lax.broadcasted_iota(jnp.int32, (tm, tm), 1)
    before = (col < row).astype(BF16)
    rank_all = jnp.dot(before, sel.astype(BF16), preferred_element_type=F32) + run_ref[...]
    run_ref[...] = run_ref[...] + jnp.sum(sel, axis=0, keepdims=True)
    cnt_ref[...] = run_ref[...].astype(jnp.int32)

    mi = jnp.zeros((tm, LANES), jnp.int32)
    mf = jnp.zeros((tm, LANES), F32)
    for k in range(TOP_K):
        rank_k = jnp.sum(jnp.where(hots[k], rank_all, 0.0), axis=-1, keepdims=True)
        mi = jnp.where(lane == k, idxs[k], mi)
        mi = jnp.where(lane == TOP_K + k, rank_k.astype(jnp.int32), mi)
        mf = jnp.where(lane == k, exps[k] / denom, mf)
    mi_ref[...] = mi
    mf_ref[...] = mf


def _router(h, g, rw, rb, *, tm=512):
    T, D = h.shape
    rw_pad = jnp.zeros((D, LANES), F32).at[:, :N_EXPERTS].set(rw)
    rb_pad = jnp.zeros((1, LANES), F32).at[0, :N_EXPERTS].set(rb)
    return pl.pallas_call(
        functools.partial(_router_body, tm=tm),
        out_shape=(jax.ShapeDtypeStruct((T, D // 2), jnp.uint32),
                   jax.ShapeDtypeStruct((T, LANES), jnp.int32),
                   jax.ShapeDtypeStruct((T, LANES), F32),
                   jax.ShapeDtypeStruct((1, LANES), jnp.int32)),
        grid=(T // tm,),
        in_specs=[pl.BlockSpec((tm, D), lambda i: (i, 0)),
                  pl.BlockSpec((1, D), lambda i: (0, 0)),
                  pl.BlockSpec((D, LANES), lambda i: (0, 0)),
                  pl.BlockSpec((1, LANES), lambda i: (0, 0))],
        out_specs=(pl.BlockSpec((tm, D // 2), lambda i: (i, 0)),
                   pl.BlockSpec((tm, LANES), lambda i: (i, 0)),
                   pl.BlockSpec((tm, LANES), lambda i: (i, 0)),
                   pl.BlockSpec((1, LANES), lambda i: (0, 0))),
        scratch_shapes=[pltpu.VMEM((1, LANES), F32)],
        compiler_params=_params(1),
        name="router",
    )(h, g.reshape(1, D), rw_pad, rb_pad)


def _dispatch_body(slot_ref, xs_ref, xb_in_ref, xb_ref, sem, *, tm):
    del xb_in_ref
    base = pl.program_id(0) * (tm * TOP_K)

    def row_copy(r, s):
        return pltpu.make_async_copy(xs_ref.at[pl.ds(r, 1), :], xb_ref.at[pl.ds(s, 1), :], sem)

    def issue(r, carry):
        for k in range(TOP_K):
            row_copy(r, slot_ref[base + r * TOP_K + k]).start(priority=k % 2)
        return carry

    lax.fori_loop(0, tm, issue, 0, unroll=8)
    for _ in range(TOP_K):
        pltpu.make_async_copy(xs_ref, xb_ref.at[pl.ds(0, tm), :], sem).wait()


def _dispatch(slot_flat, xs, n_rows, *, tm=512):
    T, D = xs.shape
    xb_init = jnp.zeros((n_rows, D), xs.dtype)
    return pl.pallas_call(
        functools.partial(_dispatch_body, tm=tm),
        out_shape=jax.ShapeDtypeStruct((n_rows, D), xs.dtype),
        grid_spec=pltpu.PrefetchScalarGridSpec(
            num_scalar_prefetch=1,
            grid=(T // tm,),
            in_specs=[pl.BlockSpec((tm, D), lambda i, s: (i, 0)),
                      pl.BlockSpec(memory_space=pl.ANY)],
            out_specs=pl.BlockSpec(memory_space=pl.ANY),
            scratch_shapes=[pltpu.SemaphoreType.DMA],
        ),
        input_output_aliases={2: 0},
        compiler_params=_params(1),
        name="moe_dispatch",
    )(slot_flat, xs, xb_init)


def _expert_steps(tiles, n_col, max_tiles):
    E = tiles.shape[0]
    tend = jnp.cumsum(tiles)
    tstart = tend - tiles
    n_used_steps = n_col * tend[-1]
    s = jnp.arange(n_col * max_tiles, dtype=jnp.int32)
    s_eff = jnp.minimum(s, jnp.maximum(n_used_steps - 1, 0))
    e_s = jnp.minimum(
        jnp.sum((n_col * tend[None, :] <= s_eff[:, None]).astype(jnp.int32), axis=1), E - 1)
    q = s_eff - n_col * tstart[e_s]
    per = jnp.maximum(tiles[e_s], 1)
    wcol = jnp.clip(q // per, 0, n_col - 1)
    r = q - wcol * per
    used = s < n_used_steps
    first = jnp.logical_and(used, r == 0)
    spare = jnp.maximum(s - n_used_steps, 0)
    col = jnp.where(used, wcol, spare % n_col)
    tile = jnp.where(used, tstart[e_s] + r, tend[-1] + spare // n_col)
    i32 = lambda a: a.astype(jnp.int32)
    return i32(e_s), i32(wcol), i32(col), i32(tile), i32(used), i32(first)


def _expert_up_body(ex_ref, wcol_ref, col_ref, tile_ref, used_ref, first_ref, x_ref, w1g_ref,
                    w1l_ref, b1g_ref, b1l_ref, h_ref, wg_ref, wl_ref):
    del ex_ref, wcol_ref, col_ref, tile_ref
    s = pl.program_id(0)

    @pl.when(used_ref[s] == 0)
    def _():
        h_ref[...] = jnp.zeros(h_ref.shape, h_ref.dtype)

    @pl.when(first_ref[s] == 1)
    def _():
        wg_ref[...] = w1g_ref[...].astype(BF16)
        wl_ref[...] = w1l_ref[...].astype(BF16)

    @pl.when(used_ref[s] == 1)
    def _():
        lo, hi = _unpack_bf16_pairs(x_ref[...])
        x = jnp.concatenate([lo, hi], axis=1)
        glu = jnp.dot(x, wg_ref[...], preferred_element_type=F32) + b1g_ref[...]
        lin = jnp.dot(x, wl_ref[...], preferred_element_type=F32) + b1l_ref[...]
        glu = jnp.minimum(glu, SWIGLU_LIMIT)
        lin = jnp.clip(lin, -SWIGLU_LIMIT, SWIGLU_LIMIT)
        h_ref[...] = (glu * _sigmoid(SWIGLU_ALPHA * glu) * (lin + 1.0)).astype(h_ref.dtype)


def _expert_down_body(ex_ref, wcol_ref, col_ref, tile_ref, used_ref, first_ref, h_ref, w2_ref,
                      b2_ref, o_ref, wd_ref):
    del ex_ref, wcol_ref, col_ref, tile_ref
    s = pl.program_id(0)

    @pl.when(used_ref[s] == 0)
    def _():
        o_ref[...] = jnp.zeros(o_ref.shape, o_ref.dtype)

    @pl.when(first_ref[s] == 1)
    def _():
        wd_ref[...] = w2_ref[...].astype(BF16)

    @pl.when(used_ref[s] == 1)
    def _():
        y = jnp.dot(h_ref[...], wd_ref[...], preferred_element_type=F32) + b2_ref[...]
        o_ref[...] = _pack_bf16_pairs(y)


def _expert_ffn(tiles, xb, w1, b1, w2, b2, layer):
    P, Dh = xb.shape
    D = 2 * Dh
    _, E, _, F2 = w1.shape
    F = F2 // 2
    tm, tc = EXPERT_TILE_M, EXPERT_TILE_COLS
    max_tiles = P // tm
    b1r = b1.reshape(-1, E, 1, F2)
    b2r = b2.reshape(-1, E, 1, D)

    nf = F // tc
    steps = _expert_steps(tiles, nf, max_tiles)
    hidden = pl.pallas_call(
        _expert_up_body,
        out_shape=jax.ShapeDtypeStruct((P, F), BF16),
        grid_spec=pltpu.PrefetchScalarGridSpec(
            num_scalar_prefetch=6,
            grid=(nf * max_tiles,),
            in_specs=[
                pl.BlockSpec((tm, Dh), lambda s, ex, wc, col, tl, us, fr: (tl[s], 0)),
                pl.BlockSpec((None, None, D, tc),
                             lambda s, ex, wc, col, tl, us, fr: (layer, ex[s], 0, wc[s])),
                pl.BlockSpec((None, None, D, tc),
                             lambda s, ex, wc, col, tl, us, fr: (layer, ex[s], 0, nf + wc[s])),
                pl.BlockSpec((None, None, 1, tc),
                             lambda s, ex, wc, col, tl, us, fr: (layer, ex[s], 0, wc[s])),
                pl.BlockSpec((None, None, 1, tc),
                             lambda s, ex, wc, col, tl, us, fr: (layer, ex[s], 0, nf + wc[s])),
            ],
            out_specs=pl.BlockSpec((tm, tc), lambda s, ex, wc, col, tl, us, fr: (tl[s], col[s])),
            scratch_shapes=[pltpu.VMEM((D, tc), BF16), pltpu.VMEM((D, tc), BF16)],
        ),
        compiler_params=_params(1, EXPERT_VMEM_LIMIT_BYTES),
        name="moe_up",
    )(*steps, xb, w1, w1, b1r, b1r)

    steps = _expert_steps(tiles, 1, max_tiles)
    return pl.pallas_call(
        _expert_down_body,
        out_shape=jax.ShapeDtypeStruct((P, Dh), jnp.uint32),
        grid_spec=pltpu.PrefetchScalarGridSpec(
            num_scalar_prefetch=6,
            grid=(max_tiles,),
            in_specs=[
                pl.BlockSpec((tm, F), lambda s, ex, wc, col, tl, us, fr: (tl[s], 0)),
                pl.BlockSpec((None, None, F, D),
                             lambda s, ex, wc, col, tl, us, fr: (layer, ex[s], 0, 0)),
                pl.BlockSpec((None, None, 1, D),
                             lambda s, ex, wc, col, tl, us, fr: (layer, ex[s], 0, 0)),
            ],
            out_specs=pl.BlockSpec((tm, Dh), lambda s, ex, wc, col, tl, us, fr: (tl[s], 0)),
            scratch_shapes=[pltpu.VMEM((F, D), BF16)],
        ),
        compiler_params=_params(1, EXPERT_VMEM_LIMIT_BYTES),
        name="moe_down",
    )(*steps, hidden, w2, b2r)


def _combine_body(slot_ref, yb_ref, gate_ref, h_ref, o_ref, buf_ref, sems, *, tm, n_steps):
    i = pl.program_id(0)

    def start_gather(step, buf):
        base = step * (tm * TOP_K)

        def issue(r, carry):
            for k in range(TOP_K):
                s = slot_ref[base + r * TOP_K + k]
                pltpu.make_async_copy(yb_ref.at[pl.ds(s, 1), :],
                                      buf_ref.at[buf, k, pl.ds(r, 1), :],
                                      sems.at[buf]).start(priority=k % 2)
            return carry

        lax.fori_loop(0, tm, issue, 0, unroll=8)

    @pl.when(i == 0)
    def _():
        start_gather(0, 0)

    for buf in range(2):
        @pl.when(jnp.logical_and(i + 1 < n_steps, (i + 1) % 2 == buf))
        def _():
            start_gather(i + 1, buf)

    cur = i % 2
    for k in range(TOP_K):
        pltpu.make_async_copy(yb_ref.at[pl.ds(0, tm), :], buf_ref.at[cur, k], sems.at[cur]).wait()
    half = buf_ref.shape[-1]
    acc_lo = h_ref[:, :half]
    acc_hi = h_ref[:, half:]
    gates = gate_ref[...]
    for k in range(TOP_K):
        lo, hi = _unpack_bf16_pairs(buf_ref[cur, k], F32)
        acc_lo = acc_lo + gates[:, k:k + 1] * lo
        acc_hi = acc_hi + gates[:, k:k + 1] * hi
    o_ref[:, :half] = acc_lo
    o_ref[:, half:] = acc_hi


def _combine(slot_flat, yb, gates, h, *, tm=256):
    T, D = h.shape
    n_steps = T // tm
    return pl.pallas_call(
        functools.partial(_combine_body, tm=tm, n_steps=n_steps),
        out_shape=jax.ShapeDtypeStruct((T, D), F32),
        grid_spec=pltpu.PrefetchScalarGridSpec(
            num_scalar_prefetch=1,
            grid=(n_steps,),
            in_specs=[pl.BlockSpec(memory_space=pl.ANY),
                      pl.BlockSpec((tm, LANES), lambda i, s: (i, 0)),
                      pl.BlockSpec((tm, D), lambda i, s: (i, 0))],
            out_specs=pl.BlockSpec((tm, D), lambda i, s: (i, 0)),
            scratch_shapes=[pltpu.VMEM((2, TOP_K, tm, D // 2), jnp.uint32),
                            pltpu.SemaphoreType.DMA((2,))],
        ),
        compiler_params=_params(1),
        name="moe_combine",
    )(slot_flat, yb, gates, h)


def _moe(h, norm_g, router_w, router_b, w1, b1, w2, b2, layer):
    T, D = h.shape
    E, tm = N_EXPERTS, EXPERT_TILE_M
    xs, meta_i, meta_f, cnt = _router(h, norm_g, router_w, router_b)
    expert = meta_i[:, :TOP_K]
    rank = meta_i[:, TOP_K:2 * TOP_K]
    counts = cnt[0, :E]
    padded = (counts + tm - 1) // tm * tm
    pend = jnp.cumsum(padded)
    pstart = pend - padded
    slot_flat = (pstart[expert] + rank).reshape(-1).astype(jnp.int32)
    max_tiles = (T * TOP_K) // tm + E
    xb = _dispatch(slot_flat, xs, max_tiles * tm)
    yb = _expert_ffn(padded // tm, xb, w1, b1, w2, b2, layer)
    return _combine(slot_flat, yb, meta_f, h)


def _ple_body(*refs, final):
    (h_ref, p_ref, wp_ref, pn_ref, gn_ref, gw_ref, gb_ref) = refs[:7]
    fn_ref = refs[7] if final else None
    o_ref = refs[-1]
    h = h_ref[...]
    e = jnp.dot(p_ref[...].astype(BF16), wp_ref[...], preferred_element_type=F32)
    e = _rms(e, pn_ref[...])
    hn = _rms(h, gn_ref[...]).astype(BF16)
    gate = _sigmoid(jnp.dot(hn, gw_ref[...], preferred_element_type=F32) + gb_ref[...])
    out = h + gate * e
    if final:
        out = _rms(out, fn_ref[...])
    o_ref[...] = out


def _ple(h, p_all, layer, w_p, p_norm, gate_norm, gate_w, gate_b, final_norm=None, *, tm=512):
    T, D = h.shape
    Pd = p_all.shape[-1]
    final = final_norm is not None
    vec = pl.BlockSpec((1, D), lambda i: (0, 0))
    in_specs = [pl.BlockSpec((tm, D), lambda i: (i, 0)),
                pl.BlockSpec((None, tm, Pd), lambda i: (layer, i, 0)),
                pl.BlockSpec((Pd, D), lambda i: (0, 0)),
                vec, vec,
                pl.BlockSpec((D, D), lambda i: (0, 0)),
                vec]
    args = [h, p_all, w_p.astype(BF16), p_norm.reshape(1, D), gate_norm.reshape(1, D),
            gate_w.astype(BF16), gate_b.reshape(1, D)]
    if final:
        in_specs.append(vec)
        args.append(final_norm.reshape(1, D))
    return pl.pallas_call(
        functools.partial(_ple_body, final=final),
        out_shape=jax.ShapeDtypeStruct((T, D), F32),
        grid=(T // tm,),
        in_specs=in_specs,
        out_specs=pl.BlockSpec((tm, D), lambda i: (i, 0)),
        compiler_params=_params(1),
        name="ple",
    )(*args)


def kernel(x, p, a_norm, a_w_in, a_ln_g, a_ln_b, a_w_s, a_b_s, a_w_out, b_norm, b_w_qkv, b_lq1, b_lk1, b_lq2, b_lk2, b_subln, b_w_out, moe_norm, router_w, router_b, moe_w1, moe_b1, moe_w2, moe_b2, ple_w, ple_norm, ple_gate_norm, ple_gate_w, ple_gate_b, final_norm):
    B, S, D = x.shape
    assert B == 1, "attention and chunked mixing treat the row axis as one sequence"
    depth = p.shape[0]
    h = x.reshape(B * S, D)
    for i in range(depth):
        j = i // 2
        if i % 2 == 0:
            z = _norm_matmul(h, a_norm[j], a_w_in[j].astype(BF16), act="gelu",
                             out_dtype=BF16, name="gmlp_in")
            h = _sgu(z, a_ln_g[j], a_ln_b[j], a_w_s[j], a_b_s[j], a_w_out[j].astype(BF16), h)
        else:
            qkv = _qkv_rope(h, b_norm[j], b_w_qkv[j].astype(BF16))
            o = _diff_attn(qkv, b_lq1[j], b_lk1[j], b_lq2[j], b_lk2[j], b_subln[j], i)
            h = _norm_matmul(o, None, b_w_out[j].astype(BF16), residual=h, name="attn_out")
        h = _moe(h, moe_norm[i], router_w[i], router_b[i], moe_w1, moe_b1, moe_w2, moe_b2, i)
        h = _ple(h, p.reshape(depth, B * S, -1), i, ple_w[i], ple_norm[i], ple_gate_norm[i],
                 ple_gate_w[i], ple_gate_b[i],
                 final_norm if i == depth - 1 else None)
    return h.reshape(B, S, D)
```

```python
import functools
import math

import jax
import jax.numpy as jnp
from jax import lax
from jax.experimental import pallas as pl
from jax.experimental.pallas import tpu as pltpu

F32 = jnp.float32
BF16 = jnp.bfloat16

RMS_EPS = 1e-6
LN_EPS = 1e-5
CHUNK = 128
GMLP_GROUPS = 16
DIFF_HEAD_DIM = 128
DIFF_V_DIM = 256
ROPE_THETA = 500000.0
ROPE_DIM = DIFF_HEAD_DIM // 4
ROPE_HALF = ROPE_DIM // 2
N_EXPERTS = 32
TOP_K = 4
SWIGLU_ALPHA = 1.702
SWIGLU_LIMIT = 7.0
Q_SCALE = DIFF_HEAD_DIM ** -0.5 * math.log2(math.e)

LANES = 128
VMEM_LIMIT_BYTES = 48 * 1024 * 1024

EXPERT_TILE_M = 512
EXPERT_TILE_COLS = 1024
EXPERT_VMEM_LIMIT_BYTES = 56 * 1024 * 1024


def _params(n_axes, vmem=VMEM_LIMIT_BYTES):
    return pltpu.CompilerParams(
        dimension_semantics=("arbitrary",) * n_axes, vmem_limit_bytes=vmem)


def _rms(x, g):
    var = jnp.mean(x * x, axis=-1, keepdims=True)
    return x * lax.rsqrt(var + RMS_EPS) * g


def _sigmoid(x):
    return 1.0 / (1.0 + jnp.exp(-x))


def _pack_bf16_pairs(x):
    n = x.shape[1] // 2
    bits = lax.bitcast_convert_type(x.astype(BF16).astype(F32), jnp.uint32)
    return (bits[:, :n] >> 16) | bits[:, n:]


def _unpack_bf16_pairs(packed, dtype=BF16):
    lo = lax.bitcast_convert_type(packed << 16, F32)
    hi = lax.bitcast_convert_type(packed & jnp.uint32(0xFFFF0000), F32)
    return lo.astype(dtype), hi.astype(dtype)


def _norm_matmul_body(*refs, norm, has_bias, act, has_res):
    it = iter(refs)
    x_ref = next(it)
    g_ref = next(it) if norm else None
    w_ref = next(it)
    b_ref = next(it) if has_bias else None
    r_ref = next(it) if has_res else None
    o_ref = next(it)
    xn_ref = next(it)

    @pl.when(pl.program_id(1) == 0)
    def _():
        x = x_ref[...].astype(F32)
        if norm:
            x = _rms(x, g_ref[...])
        xn_ref[...] = x.astype(BF16)

    acc = jnp.dot(xn_ref[...], w_ref[...], preferred_element_type=F32)
    if has_bias:
        acc = acc + b_ref[...]
    if act == "gelu":
        acc = 0.5 * acc * (1.0 + lax.erf(acc * (1.0 / math.sqrt(2.0))))
    if has_res:
        acc = acc + r_ref[...]
    o_ref[...] = acc.astype(o_ref.dtype)


def _norm_matmul(x, g, w, *, bias=None, act=None, residual=None, out_dtype=F32,
                 tm=1024, tn=1024, name="norm_matmul"):
    M, K = x.shape
    N = w.shape[1]
    norm = g is not None
    in_specs = [pl.BlockSpec((tm, K), lambda i, j: (i, 0))]
    args = [x]
    if norm:
        in_specs.append(pl.BlockSpec((1, K), lambda i, j: (0, 0)))
        args.append(g.reshape(1, K))
    in_specs.append(pl.BlockSpec((K, tn), lambda i, j: (0, j)))
    args.append(w)
    if bias is not None:
        in_specs.append(pl.BlockSpec((1, tn), lambda i, j: (0, j)))
        args.append(bias.reshape(1, N))
    if residual is not None:
        in_specs.append(pl.BlockSpec((tm, tn), lambda i, j: (i, j)))
        args.append(residual)
    body = functools.partial(_norm_matmul_body, norm=norm, has_bias=bias is not None,
                             act=act, has_res=residual is not None)
    return pl.pallas_call(
        body,
        out_shape=jax.ShapeDtypeStruct((M, N), out_dtype),
        grid=(M // tm, N // tn),
        in_specs=in_specs,
        out_specs=pl.BlockSpec((tm, tn), lambda i, j: (i, j)),
        scratch_shapes=[pltpu.VMEM((tm, K), BF16)],
        compiler_params=_params(2),
        name=name,
    )(*args)


def _sgu_body(u_ref, v_ref, lng_ref, lnb_ref, ws_ref, bias_ref, wo_ref, h_ref, o_ref,
              wt_ref, y_ref, *, tm):
    @pl.when(pl.program_id(0) == 0)
    def _():
        row = lax.broadcasted_iota(jnp.int32, (CHUNK, CHUNK), 0)
        col = lax.broadcasted_iota(jnp.int32, (CHUNK, CHUNK), 1)
        causal = col <= row
        for gi in range(GMLP_GROUPS):
            wt_ref[gi] = jnp.where(causal, ws_ref[gi], 0.0).astype(BF16)

    v = v_ref[...].astype(F32)
    mu = jnp.mean(v, axis=-1, keepdims=True)
    vc = v - mu
    var = jnp.mean(vc * vc, axis=-1, keepdims=True)
    vn = (vc * lax.rsqrt(var + LN_EPS) * lng_ref[...] + lnb_ref[...]).astype(BF16)
    for c in range(tm // CHUNK):
        rows = slice(c * CHUNK, (c + 1) * CHUNK)
        for gi in range(GMLP_GROUPS):
            cols = slice(gi * LANES, (gi + 1) * LANES)
            sv = jnp.dot(wt_ref[gi], vn[rows, cols], preferred_element_type=F32)
            sv = sv + bias_ref[:, cols]
            y_ref[rows, cols] = (u_ref[rows, cols].astype(F32) * sv).astype(BF16)
    o_ref[...] = h_ref[...] + jnp.dot(y_ref[...], wo_ref[...], preferred_element_type=F32)


def _sgu(z, ln_g, ln_b, w_s, b_s, w_out, h, *, tm=256):
    T, D = h.shape
    W = z.shape[1] // 2
    bias_tile = jnp.repeat(b_s.T, W // GMLP_GROUPS, axis=1)
    return pl.pallas_call(
        functools.partial(_sgu_body, tm=tm),
        out_shape=jax.ShapeDtypeStruct((T, D), F32),
        grid=(T // tm,),
        in_specs=[
            pl.BlockSpec((tm, W), lambda i: (i, 0)),
            pl.BlockSpec((tm, W), lambda i: (i, 1)),
            pl.BlockSpec((1, W), lambda i: (0, 0)),
            pl.BlockSpec((1, W), lambda i: (0, 0)),
            pl.BlockSpec((GMLP_GROUPS, CHUNK, CHUNK), lambda i: (0, 0, 0)),
            pl.BlockSpec((CHUNK, W), lambda i: (0, 0)),
            pl.BlockSpec((W, D), lambda i: (0, 0)),
            pl.BlockSpec((tm, D), lambda i: (i, 0)),
        ],
        out_specs=pl.BlockSpec((tm, D), lambda i: (i, 0)),
        scratch_shapes=[pltpu.VMEM((GMLP_GROUPS, CHUNK, CHUNK), BF16),
                        pltpu.VMEM((tm, W), BF16)],
        compiler_params=_params(1),
        name="sgu",
    )(z, z, ln_g.reshape(1, W), ln_b.reshape(1, W), w_s, bias_tile, w_out, h)


def _qkv_body(x_ref, g_ref, w_ref, c_ref, s1_ref, s2_ref, o_ref, xn_ref, *,
              n_q_tiles, n_qk_tiles, heads_per_tile):
    j = pl.program_id(1)

    @pl.when(j == 0)
    def _():
        xn_ref[...] = _rms(x_ref[...], g_ref[...]).astype(BF16)

    is_v = j >= n_qk_tiles
    scale = jnp.where(j < n_q_tiles, Q_SCALE, 1.0).astype(F32)
    cosf = jnp.where(is_v, 1.0, c_ref[...] * scale)
    s1 = jnp.where(is_v, 0.0, s1_ref[...] * scale)
    s2 = jnp.where(is_v, 0.0, s2_ref[...] * scale)
    acc = jnp.dot(xn_ref[...], w_ref[...], preferred_element_type=F32)
    for hh in range(heads_per_tile):
        cols = slice(hh * LANES, (hh + 1) * LANES)
        seg = acc[:, cols]
        up = pltpu.roll(seg, LANES - ROPE_HALF, 1)
        dn = pltpu.roll(seg, ROPE_HALF, 1)
        o_ref[:, cols] = (seg * cosf + up * s1 + dn * s2).astype(o_ref.dtype)


def _rope_tables(T):
    lane = jnp.arange(LANES, dtype=jnp.int32)[None, :]
    inv_freq = jnp.power(ROPE_THETA, -(2.0 * (lane % ROPE_HALF).astype(F32)) / ROPE_DIM)
    ang = jnp.arange(T, dtype=F32)[:, None] * inv_freq
    cos, sin = jnp.cos(ang), jnp.sin(ang)
    c_tab = jnp.where(lane < ROPE_DIM, cos, 1.0)
    s1_tab = jnp.where(lane < ROPE_HALF, -sin, 0.0)
    s2_tab = jnp.where(jnp.logical_and(lane >= ROPE_HALF, lane < ROPE_DIM), sin, 0.0)
    return c_tab, s1_tab, s2_tab


def _qkv_rope(h, g, w, *, tm=1024, tn=512):
    T, D = h.shape
    N = w.shape[1]
    c_tab, s1_tab, s2_tab = _rope_tables(T)
    tab_spec = pl.BlockSpec((tm, LANES), lambda i, j: (i, 0))
    body = functools.partial(_qkv_body, n_q_tiles=D // tn, n_qk_tiles=2 * D // tn,
                             heads_per_tile=tn // LANES)
    return pl.pallas_call(
        body,
        out_shape=jax.ShapeDtypeStruct((T, N), BF16),
        grid=(T // tm, N // tn),
        in_specs=[pl.BlockSpec((tm, D), lambda i, j: (i, 0)),
                  pl.BlockSpec((1, D), lambda i, j: (0, 0)),
                  pl.BlockSpec((D, tn), lambda i, j: (0, j)),
                  tab_spec, tab_spec, tab_spec],
        out_specs=pl.BlockSpec((tm, tn), lambda i, j: (i, j)),
        scratch_shapes=[pltpu.VMEM((tm, D), BF16)],
        compiler_params=_params(2),
        name="qkv_rope",
    )(h, g.reshape(1, D), w, c_tab, s1_tab, s2_tab)


def _diff_attn_body(lq1_ref, lk1_ref, lq2_ref, lk2_ref, g_ref, q_ref, k_ref, v_ref, o_ref,
                    m0_ref, l0_ref, acc0_ref, m1_ref, l1_ref, acc1_ref, *, tq, lambda_init):
    qi = pl.program_id(1)
    Dh = DIFF_HEAD_DIM
    state = ((m0_ref, l0_ref, acc0_ref), (m1_ref, l1_ref, acc1_ref))
    for m_ref, l_ref, acc_ref in state:
        m_ref[...] = jnp.full(m_ref.shape, -jnp.inf, F32)
        l_ref[...] = jnp.zeros(l_ref.shape, F32)
        acc_ref[...] = jnp.zeros(acc_ref.shape, F32)

    n_rep = tq // LANES

    def chunk(j, masked):
        start = pl.multiple_of(j * tq, tq)
        k = k_ref[pl.ds(start, tq), :]
        v = v_ref[pl.ds(start, tq), :]
        scores = []
        for c in range(2):
            s = lax.dot_general(q_ref[:, c * Dh:(c + 1) * Dh], k[:, c * Dh:(c + 1) * Dh],
                                (((1,), (1,)), ((), ())), preferred_element_type=F32)
            if masked:
                row = lax.broadcasted_iota(jnp.int32, (tq, tq), 0)
                col = lax.broadcasted_iota(jnp.int32, (tq, tq), 1)
                s = jnp.where(col <= row, s, -jnp.inf)
            scores.append(s)
        for s, (m_ref, l_ref, acc_ref) in zip(scores, state):
            m_prev = m_ref[...]
            m_new = jnp.maximum(m_prev, jnp.max(s, axis=-1, keepdims=True))
            p = jnp.exp2(s - jnp.concatenate([m_new] * n_rep, axis=1))
            alpha = jnp.exp2(m_prev - m_new)
            psum = p[:, :LANES]
            for r in range(1, n_rep):
                psum = psum + p[:, r * LANES:(r + 1) * LANES]
            l_ref[...] = alpha * l_ref[...] + psum
            acc_ref[...] = (jnp.concatenate([alpha] * (DIFF_V_DIM // LANES), axis=1) * acc_ref[...]
                            + jnp.dot(p.astype(BF16), v, preferred_element_type=F32))
            m_ref[...] = m_new

    def full_chunk(j, carry):
        chunk(j, False)
        return carry

    lax.fori_loop(0, qi, full_chunk, 0)
    chunk(qi, True)

    lam = (jnp.exp(jnp.sum(lq1_ref[...] * lk1_ref[...], axis=-1, keepdims=True))
           - jnp.exp(jnp.sum(lq2_ref[...] * lk2_ref[...], axis=-1, keepdims=True))
           + lambda_init)
    l0 = jnp.sum(l0_ref[...], axis=-1, keepdims=True)
    l1 = jnp.sum(l1_ref[...], axis=-1, keepdims=True)
    o = acc0_ref[...] / l0 - lam * (acc1_ref[...] / l1)
    o_ref[...] = (_rms(o, g_ref[...]) * (1.0 - lambda_init)).astype(o_ref.dtype)


def _diff_attn(qkv, lq1, lk1, lq2, lk2, subln_g, layer_idx, *, tq=1024):
    T = qkv.shape[0]
    Dv = DIFF_V_DIM
    D = qkv.shape[1] // 3
    H = D // Dv
    lambda_init = 0.8 - 0.6 * math.exp(-0.3 * layer_idx)
    vec = pl.BlockSpec((1, DIFF_HEAD_DIM), lambda h, i: (0, 0))
    body = functools.partial(_diff_attn_body, tq=tq, lambda_init=lambda_init)
    return pl.pallas_call(
        body,
        out_shape=jax.ShapeDtypeStruct((T, D), BF16),
        grid=(H, T // tq),
        in_specs=[vec, vec, vec, vec,
                  pl.BlockSpec((1, Dv), lambda h, i: (0, 0)),
                  pl.BlockSpec((tq, Dv), lambda h, i: (i, h)),
                  pl.BlockSpec((T, Dv), lambda h, i: (0, H + h),
                               pipeline_mode=pl.Buffered(1)),
                  pl.BlockSpec((T, Dv), lambda h, i: (0, 2 * H + h),
                               pipeline_mode=pl.Buffered(1))],
        out_specs=pl.BlockSpec((tq, Dv), lambda h, i: (i, h)),
        scratch_shapes=[pltpu.VMEM((tq, LANES), F32), pltpu.VMEM((tq, LANES), F32),
                        pltpu.VMEM((tq, Dv), F32)] * 2,
        compiler_params=_params(2),
        name="diff_attn",
    )(lq1.reshape(1, -1), lk1.reshape(1, -1), lq2.reshape(1, -1), lk2.reshape(1, -1),
      subln_g.reshape(1, Dv), qkv, qkv, qkv)


def _router_body(h_ref, g_ref, rw_ref, rb_ref, xs_ref, mi_ref, mf_ref, cnt_ref, run_ref, *, tm):
    @pl.when(pl.program_id(0) == 0)
    def _():
        run_ref[...] = jnp.zeros(run_ref.shape, F32)

    xs = _rms(h_ref[...], g_ref[...])
    xs_ref[...] = _pack_bf16_pairs(xs)
    xs_hi = xs.astype(BF16)
    xs_lo = (xs - xs_hi.astype(F32)).astype(BF16)
    rw_hi = rw_ref[...].astype(BF16)
    rw_lo = (rw_ref[...] - rw_hi.astype(F32)).astype(BF16)
    logits = (jnp.dot(xs_hi, rw_hi, preferred_element_type=F32)
              + jnp.dot(xs_hi, rw_lo, preferred_element_type=F32)
              + jnp.dot(xs_lo, rw_hi, preferred_element_type=F32)) + rb_ref[...]
    lane = lax.broadcasted_iota(jnp.int32, (tm, LANES), 1)
    work = jnp.where(lane < N_EXPERTS, logits, -jnp.inf)
    vals, idxs, hots = [], [], []
    for _ in range(TOP_K):
        mx = jnp.max(work, axis=-1, keepdims=True)
        idx = jnp.min(jnp.where(work == mx, lane, LANES), axis=-1, keepdims=True)
        hot = lane == idx
        vals.append(mx)
        idxs.append(idx)
        hots.append(hot)
        work = jnp.where(hot, -jnp.inf, work)
    exps = [jnp.exp(v - vals[0]) for v in vals]
    denom = exps[0] + exps[1] + exps[2] + exps[3]

    sel = jnp.zeros((tm, LANES), F32)
    for hot in hots:
        sel = sel + hot.astype(F32)
    row = lax.broadcasted_iota(jnp.int32, (tm, tm), 0)
    col = lax.broadcasted_iota(jnp.int32, (tm, tm), 1)
    before = (col < row).astype(BF16)
    rank_all = jnp.dot(before, sel.astype(BF16), preferred_element_type=F32) + run_ref[...]
    run_ref[...] = run_ref[...] + jnp.sum(sel, axis=0, keepdims=True)
    cnt_ref[...] = run_ref[...].astype(jnp.int32)

    mi = jnp.zeros((tm, LANES), jnp.int32)
    mf = jnp.zeros((tm, LANES), F32)
    for k in range(TOP_K):
        rank_k = jnp.sum(jnp.where(hots[k], rank_all, 0.0), axis=-1, keepdims=True)
        mi = jnp.where(lane == k, idxs[k], mi)
        mi = jnp.where(lane == TOP_K + k, rank_k.astype(jnp.int32), mi)
        mf = jnp.where(lane == k, exps[k] / denom, mf)
    mi_ref[...] = mi
    mf_ref[...] = mf


def _router(h, g, rw, rb, *, tm=512):
    T, D = h.shape
    rw_pad = jnp.zeros((D, LANES), F32).at[:, :N_EXPERTS].set(rw)
    rb_pad = jnp.zeros((1, LANES), F32).at[0, :N_EXPERTS].set(rb)
    return pl.pallas_call(
        functools.partial(_router_body, tm=tm),
        out_shape=(jax.ShapeDtypeStruct((T, D // 2), jnp.uint32),
                   jax.ShapeDtypeStruct((T, LANES), jnp.int32),
                   jax.ShapeDtypeStruct((T, LANES), F32),
                   jax.ShapeDtypeStruct((1, LANES), jnp.int32)),
        grid=(T // tm,),
        in_specs=[pl.BlockSpec((tm, D), lambda i: (i, 0)),
                  pl.BlockSpec((1, D), lambda i: (0, 0)),
                  pl.BlockSpec((D, LANES), lambda i: (0, 0)),
                  pl.BlockSpec((1, LANES), lambda i: (0, 0))],
        out_specs=(pl.BlockSpec((tm, D // 2), lambda i: (i, 0)),
                   pl.BlockSpec((tm, LANES), lambda i: (i, 0)),
                   pl.BlockSpec((tm, LANES), lambda i: (i, 0)),
                   pl.BlockSpec((1, LANES), lambda i: (0, 0))),
        scratch_shapes=[pltpu.VMEM((1, LANES), F32)],
        compiler_params=_params(1),
        name="router",
    )(h, g.reshape(1, D), rw_pad, rb_pad)


def _dispatch_body(slot_ref, xs_ref, xb_in_ref, xb_ref, sem, *, tm):
    del xb_in_ref
    base = pl.program_id(0) * (tm * TOP_K)

    def row_copy(r, s):
        return pltpu.make_async_copy(xs_ref.at[pl.ds(r, 1), :], xb_ref.at[pl.ds(s, 1), :], sem)

    def issue(r, carry):
        for k in range(TOP_K):
            row_copy(r, slot_ref[base + r * TOP_K + k]).start(priority=k % 2)
        return carry

    lax.fori_loop(0, tm, issue, 0, unroll=8)
    for _ in range(TOP_K):
        pltpu.make_async_copy(xs_ref, xb_ref.at[pl.ds(0, tm), :], sem).wait()


def _dispatch(slot_flat, xs, n_rows, *, tm=512):
    T, D = xs.shape
    xb_init = jnp.zeros((n_rows, D), xs.dtype)
    return pl.pallas_call(
        functools.partial(_dispatch_body, tm=tm),
        out_shape=jax.ShapeDtypeStruct((n_rows, D), xs.dtype),
        grid_spec=pltpu.PrefetchScalarGridSpec(
            num_scalar_prefetch=1,
            grid=(T // tm,),
            in_specs=[pl.BlockSpec((tm, D), lambda i, s: (i, 0)),
                      pl.BlockSpec(memory_space=pl.ANY)],
            out_specs=pl.BlockSpec(memory_space=pl.ANY),
            scratch_shapes=[pltpu.SemaphoreType.DMA],
        ),
        input_output_aliases={2: 0},
        compiler_params=_params(1),
        name="moe_dispatch",
    )(slot_flat, xs, xb_init)


def _expert_steps(tiles, n_col, max_tiles):
    E = tiles.shape[0]
    tend = jnp.cumsum(tiles)
    tstart = tend - tiles
    n_used_steps = n_col * tend[-1]
    s = jnp.arange(n_col * max_tiles, dtype=jnp.int32)
    s_eff = jnp.minimum(s, jnp.maximum(n_used_steps - 1, 0))
    e_s = jnp.minimum(
        jnp.sum((n_col * tend[None, :] <= s_eff[:, None]).astype(jnp.int32), axis=1), E - 1)
    q = s_eff - n_col * tstart[e_s]
    per = jnp.maximum(tiles[e_s], 1)
    wcol = jnp.clip(q // per, 0, n_col - 1)
    r = q - wcol * per
    used = s < n_used_steps
    first = jnp.logical_and(used, r == 0)
    spare = jnp.maximum(s - n_used_steps, 0)
    col = jnp.where(used, wcol, spare % n_col)
    tile = jnp.where(used, tstart[e_s] + r, tend[-1] + spare // n_col)
    i32 = lambda a: a.astype(jnp.int32)
    return i32(e_s), i32(wcol), i32(col), i32(tile), i32(used), i32(first)


def _expert_up_body(ex_ref, wcol_ref, col_ref, tile_ref, used_ref, first_ref, x_ref, w1g_ref,
                    w1l_ref, b1g_ref, b1l_ref, h_ref, wg_ref, wl_ref):
    del ex_ref, wcol_ref, col_ref, tile_ref
    s = pl.program_id(0)

    @pl.when(used_ref[s] == 0)
    def _():
        h_ref[...] = jnp.zeros(h_ref.shape, h_ref.dtype)

    @pl.when(first_ref[s] == 1)
    def _():
        wg_ref[...] = w1g_ref[...].astype(BF16)
        wl_ref[...] = w1l_ref[...].astype(BF16)

    @pl.when(used_ref[s] == 1)
    def _():
        lo, hi = _unpack_bf16_pairs(x_ref[...])
        x = jnp.concatenate([lo, hi], axis=1)
        glu = jnp.dot(x, wg_ref[...], preferred_element_type=F32) + b1g_ref[...]
        lin = jnp.dot(x, wl_ref[...], preferred_element_type=F32) + b1l_ref[...]
        glu = jnp.minimum(glu, SWIGLU_LIMIT)
        lin = jnp.clip(lin, -SWIGLU_LIMIT, SWIGLU_LIMIT)
        h_ref[...] = (glu * _sigmoid(SWIGLU_ALPHA * glu) * (lin + 1.0)).astype(h_ref.dtype)


def _expert_down_body(ex_ref, wcol_ref, col_ref, tile_ref, used_ref, first_ref, h_ref, w2_ref,
                      b2_ref, o_ref, wd_ref):
    del ex_ref, wcol_ref, col_ref, tile_ref
    s = pl.program_id(0)

    @pl.when(used_ref[s] == 0)
    def _():
        o_ref[...] = jnp.zeros(o_ref.shape, o_ref.dtype)

    @pl.when(first_ref[s] == 1)
    def _():
        wd_ref[...] = w2_ref[...].astype(BF16)

    @pl.when(used_ref[s] == 1)
    def _():
        y = jnp.dot(h_ref[...], wd_ref[...], preferred_element_type=F32) + b2_ref[...]
        o_ref[...] = _pack_bf16_pairs(y)


def _expert_ffn(tiles, xb, w1, b1, w2, b2, layer):
    P, Dh = xb.shape
    D = 2 * Dh
    _, E, _, F2 = w1.shape
    F = F2 // 2
    tm, tc = EXPERT_TILE_M, EXPERT_TILE_COLS
    max_tiles = P // tm
    b1r = b1.reshape(-1, E, 1, F2)
    b2r = b2.reshape(-1, E, 1, D)

    nf = F // tc
    steps = _expert_steps(tiles, nf, max_tiles)
    hidden = pl.pallas_call(
        _expert_up_body,
        out_shape=jax.ShapeDtypeStruct((P, F), BF16),
        grid_spec=pltpu.PrefetchScalarGridSpec(
            num_scalar_prefetch=6,
            grid=(nf * max_tiles,),
            in_specs=[
                pl.BlockSpec((tm, Dh), lambda s, ex, wc, col, tl, us, fr: (tl[s], 0)),
                pl.BlockSpec((None, None, D, tc),
                             lambda s, ex, wc, col, tl, us, fr: (layer, ex[s], 0, wc[s])),
                pl.BlockSpec((None, None, D, tc),
                             lambda s, ex, wc, col, tl, us, fr: (layer, ex[s], 0, nf + wc[s])),
                pl.BlockSpec((None, None, 1, tc),
                             lambda s, ex, wc, col, tl, us, fr: (layer, ex[s], 0, wc[s])),
                pl.BlockSpec((None, None, 1, tc),
                             lambda s, ex, wc, col, tl, us, fr: (layer, ex[s], 0, nf + wc[s])),
            ],
            out_specs=pl.BlockSpec((tm, tc), lambda s, ex, wc, col, tl, us, fr: (tl[s], col[s])),
            scratch_shapes=[pltpu.VMEM((D, tc), BF16), pltpu.VMEM((D, tc), BF16)],
        ),
        compiler_params=_params(1, EXPERT_VMEM_LIMIT_BYTES),
        name="moe_up",
    )(*steps, xb, w1, w1, b1r, b1r)

    steps = _expert_steps(tiles, 1, max_tiles)
    return pl.pallas_call(
        _expert_down_body,
        out_shape=jax.ShapeDtypeStruct((P, Dh), jnp.uint32),
        grid_spec=pltpu.PrefetchScalarGridSpec(
            num_scalar_prefetch=6,
            grid=(max_tiles,),
            in_specs=[
                pl.BlockSpec((tm, F), lambda s, ex, wc, col, tl, us, fr: (tl[s], 0)),
                pl.BlockSpec((None, None, F, D),
                             lambda s, ex, wc, col, tl, us, fr: (layer, ex[s], 0, 0)),
                pl.BlockSpec((None, None, 1, D),
                             lambda s, ex, wc, col, tl, us, fr: (layer, ex[s], 0, 0)),
            ],
            out_specs=pl.BlockSpec((tm, Dh), lambda s, ex, wc, col, tl, us, fr: (tl[s], 0)),
            scratch_shapes=[pltpu.VMEM((F, D), BF16)],
        ),
        compiler_params=_params(1, EXPERT_VMEM_LIMIT_BYTES),
        name="moe_down",
    )(*steps, hidden, w2, b2r)


def _combine_body(slot_ref, yb_ref, gate_ref, h_ref, o_ref, buf_ref, sems, *, tm, n_steps):
    i = pl.program_id(0)

    def start_gather(step, buf):
        base = step * (tm * TOP_K)

        def issue(r, carry):
            for k in range(TOP_K):
                s = slot_ref[base + r * TOP_K + k]
                pltpu.make_async_copy(yb_ref.at[pl.ds(s, 1), :],
                                      buf_ref.at[buf, k, pl.ds(r, 1), :],
                                      sems.at[buf]).start(priority=k % 2)
            return carry

        lax.fori_loop(0, tm, issue, 0, unroll=8)

    @pl.when(i == 0)
    def _():
        start_gather(0, 0)

    for buf in range(2):
        @pl.when(jnp.logical_and(i + 1 < n_steps, (i + 1) % 2 == buf))
        def _():
            start_gather(i + 1, buf)

    cur = i % 2
    for k in range(TOP_K):
        pltpu.make_async_copy(yb_ref.at[pl.ds(0, tm), :], buf_ref.at[cur, k], sems.at[cur]).wait()
    half = buf_ref.shape[-1]
    acc_lo = h_ref[:, :half]
    acc_hi = h_ref[:, half:]
    gates = gate_ref[...]
    for k in range(TOP_K):
        lo, hi = _unpack_bf16_pairs(buf_ref[cur, k], F32)
        acc_lo = acc_lo + gates[:, k:k + 1] * lo
        acc_hi = acc_hi + gates[:, k:k + 1] * hi
    o_ref[:, :half] = acc_lo
    o_ref[:, half:] = acc_hi


def _combine(slot_flat, yb, gates, h, *, tm=256):
    T, D = h.shape
    n_steps = T // tm
    return pl.pallas_call(
        functools.partial(_combine_body, tm=tm, n_steps=n_steps),
        out_shape=jax.ShapeDtypeStruct((T, D), F32),
        grid_spec=pltpu.PrefetchScalarGridSpec(
            num_scalar_prefetch=1,
            grid=(n_steps,),
            in_specs=[pl.BlockSpec(memory_space=pl.ANY),
                      pl.BlockSpec((tm, LANES), lambda i, s: (i, 0)),
                      pl.BlockSpec((tm, D), lambda i, s: (i, 0))],
            out_specs=pl.BlockSpec((tm, D), lambda i, s: (i, 0)),
            scratch_shapes=[pltpu.VMEM((2, TOP_K, tm, D // 2), jnp.uint32),
                            pltpu.SemaphoreType.DMA((2,))],
        ),
        compiler_params=_params(1),
        name="moe_combine",
    )(slot_flat, yb, gates, h)


def _moe(h, norm_g, router_w, router_b, w1, b1, w2, b2, layer):
    T, D = h.shape
    E, tm = N_EXPERTS, EXPERT_TILE_M
    xs, meta_i, meta_f, cnt = _router(h, norm_g, router_w, router_b)
    expert = meta_i[:, :TOP_K]
    rank = meta_i[:, TOP_K:2 * TOP_K]
    counts = cnt[0, :E]
    padded = (counts + tm - 1) // tm * tm
    pend = jnp.cumsum(padded)
    pstart = pend - padded
    slot_flat = (pstart[expert] + rank).reshape(-1).astype(jnp.int32)
    max_tiles = (T * TOP_K) // tm + E
    xb = _dispatch(slot_flat, xs, max_tiles * tm)
    yb = _expert_ffn(padded // tm, xb, w1, b1, w2, b2, layer)
    return _combine(slot_flat, yb, meta_f, h)


def _ple_body(*refs, final):
    (h_ref, p_ref, wp_ref, pn_ref, gn_ref, gw_ref, gb_ref) = refs[:7]
    fn_ref = refs[7] if final else None
    o_ref = refs[-1]
    h = h_ref[...]
    e = jnp.dot(p_ref[...].astype(BF16), wp_ref[...], preferred_element_type=F32)
    e = _rms(e, pn_ref[...])
    hn = _rms(h, gn_ref[...]).astype(BF16)
    gate = _sigmoid(jnp.dot(hn, gw_ref[...], preferred_element_type=F32) + gb_ref[...])
    out = h + gate * e
    if final:
        out = _rms(out, fn_ref[...])
    o_ref[...] = out


def _ple(h, p_all, layer, w_p, p_norm, gate_norm, gate_w, gate_b, final_norm=None, *, tm=512):
    T, D = h.shape
    Pd = p_all.shape[-1]
    final = final_norm is not None
    vec = pl.BlockSpec((1, D), lambda i: (0, 0))
    in_specs = [pl.BlockSpec((tm, D), lambda i: (i, 0)),
                pl.BlockSpec((None, tm, Pd), lambda i: (layer, i, 0)),
                pl.BlockSpec((Pd, D), lambda i: (0, 0)),
                vec, vec,
                pl.BlockSpec((D, D), lambda i: (0, 0)),
                vec]
    args = [h, p_all, w_p.astype(BF16), p_norm.reshape(1, D), gate_norm.reshape(1, D),
            gate_w.astype(BF16), gate_b.reshape(1, D)]
    if final:
        in_specs.append(vec)
        args.append(final_norm.reshape(1, D))
    return pl.pallas_call(
        functools.partial(_ple_body, final=final),
        out_shape=jax.ShapeDtypeStruct((T, D), F32),
        grid=(T // tm,),
        in_specs=in_specs,
        out_specs=pl.BlockSpec((tm, D), lambda i: (i, 0)),
        compiler_params=_params(1),
        name="ple",
    )(*args)


def kernel(x, p, a_norm, a_w_in, a_ln_g, a_ln_b, a_w_s, a_b_s, a_w_out, b_norm, b_w_qkv, b_lq1, b_lk1, b_lq2, b_lk2, b_subln, b_w_out, moe_norm, router_w, router_b, moe_w1, moe_b1, moe_w2, moe_b2, ple_w, ple_norm, ple_gate_norm, ple_gate_w, ple_gate_b, final_norm):
    B, S, D = x.shape
    assert B == 1, "attention and chunked mixing treat the row axis as one sequence"
    depth = p.shape[0]
    h = x.reshape(B * S, D)
    for i in range(depth):
        j = i // 2
        if i % 2 == 0:
            z = _norm_matmul(h, a_norm[j], a_w_in[j].astype(BF16), act="gelu",
                             out_dtype=BF16, name="gmlp_in")
            h = _sgu(z, a_ln_g[j], a_ln_b[j], a_w_s[j], a_b_s[j], a_w_out[j].astype(BF16), h)
        else:
            qkv = _qkv_rope(h, b_norm[j], b_w_qkv[j].astype(BF16))
            o = _diff_attn(qkv, b_lq1[j], b_lk1[j], b_lq2[j], b_lk2[j], b_subln[j], i)
            h = _norm_matmul(o, None, b_w_out[j].astype(BF16), residual=h, name="attn_out")
        h = _moe(h, moe_norm[i], router_w[i], router_b[i], moe_w1, moe_b1, moe_w2, moe_b2, i)
        h = _ple(h, p.reshape(depth, B * S, -1), i, ple_w[i], ple_norm[i], ple_gate_norm[i],
                 ple_gate_w[i], ple_gate_b[i],
                 final_norm if i == depth - 1 else None)
    return h.reshape(B, S, D)
```
